```python
import math
import jax, jax.numpy as jnp
from jax import lax
import numpy as np

D_MODEL = 1024
BATCH = 2
SEQ = 8192
DEPTH = 2
DEC_BATCH = 32
DEC_SEQ = 4
PAST_LEN = 16384
PAGE_SIZE = 128

HEAD_DIM = 64
MOBA_HEADS = D_MODEL // 128
MOBA_BLOCK = 256
MOBA_TOPK = 3
DIFF_HEADS = D_MODEL // 256
DIFF_DIM = 64
DIFF_VDIM = 2 * DIFF_DIM
MEM_HEADS = 4
MEM_DIM = D_MODEL // 8
MEM_LEN = 256
MOBA_W = MOBA_HEADS * HEAD_DIM
DIFF_QK_W = DIFF_HEADS * 2 * DIFF_DIM
DIFF_V_W = DIFF_HEADS * DIFF_VDIM
MEM_W = MEM_HEADS * MEM_DIM
BRANCH_W = 512
N_BRANCH = 3
D_IN = 3 * MOBA_W + 2 * DIFF_QK_W + DIFF_V_W + MEM_W
SPLITS = (MOBA_W, 2 * MOBA_W, 3 * MOBA_W, 3 * MOBA_W + DIFF_QK_W,
          3 * MOBA_W + 2 * DIFF_QK_W, 3 * MOBA_W + 2 * DIFF_QK_W + DIFF_V_W)
D_FF = 4 * D_MODEL
ROPE_THETA = 10000.0
Q_BLOCK = 128
NORM_EPS = 1e-6
NEG = -1e30

kernel_name = 'moba_diff_mem_gated_hybrid_step'


def rmsnorm(x, g):
    xf = x.astype(jnp.float32)
    y = xf * lax.rsqrt(jnp.mean(xf * xf, axis=-1, keepdims=True) + NORM_EPS)
    return (y * g.astype(jnp.float32)).astype(x.dtype)


def rope(x, pos):
    d = x.shape[-1]
    half = d // 2
    inv = ROPE_THETA ** (-2.0 * jnp.arange(half, dtype=jnp.float32) / d)
    ang = pos.astype(jnp.float32)[:, None] * inv[None, :]
    shp = (pos.shape[0],) + (1,) * (x.ndim - 3) + (half,)
    cos = jnp.cos(ang).reshape(shp).astype(x.dtype)
    sin = jnp.sin(ang).reshape(shp).astype(x.dtype)
    x1, x2 = x[..., :half], x[..., half:]
    return jnp.concatenate([x1 * cos - x2 * sin, x2 * cos + x1 * sin], axis=-1)


def project_heads(h, p, pos):
    B, T, _ = h.shape
    z = h @ p['w_in']
    mq, mk, mv, dq, dk, dv, cq = jnp.split(z, SPLITS, axis=-1)
    mq = rope(rmsnorm(mq.reshape(B, T, MOBA_HEADS, HEAD_DIM), p['moba_qn']), pos)
    mk = rope(rmsnorm(mk.reshape(B, T, MOBA_HEADS, HEAD_DIM), p['moba_kn']), pos)
    mv = mv.reshape(B, T, MOBA_HEADS, HEAD_DIM)
    dq = rope(rmsnorm(dq.reshape(B, T, DIFF_HEADS, 2, DIFF_DIM), p['diff_qn']), pos)
    dk = rope(rmsnorm(dk.reshape(B, T, DIFF_HEADS, 2, DIFF_DIM), p['diff_kn']), pos)
    dv = dv.reshape(B, T, DIFF_HEADS, DIFF_VDIM)
    cq = rmsnorm(cq.reshape(B, T, MEM_HEADS, MEM_DIM), p['mem_qn'])
    return mq, mk, mv, dq, dk, dv, cq


def gather_blocks(xb, idx):
    B, Q, H, n = idx.shape
    b = jnp.arange(B)[:, None, None, None, None]
    hh = jnp.arange(H)[None, None, :, None, None]
    r = jnp.arange(MOBA_BLOCK)
    g = xb[b, idx[..., None], r, hh]
    return g.reshape(B, Q, H, n * MOBA_BLOCK, xb.shape[-1])


def moba_attend(q, n_full, k_mean, kb, vb, n_sel, own_segs):
    scale = HEAD_DIM ** -0.5
    logits, parts = [], []
    if n_sel > 0:
        gate = jnp.einsum('bqhd,bnhd->bqhn', q.astype(jnp.float32), k_mean)
        blk = jnp.arange(k_mean.shape[1])
        gate = jnp.where(blk < n_full, gate, NEG)
        _, idx = lax.top_k(gate, n_sel)
        valid = jnp.repeat(jnp.arange(n_sel) < n_full, MOBA_BLOCK)
        k_sel = gather_blocks(kb, idx)
        v_sel = gather_blocks(vb, idx)
        s = jnp.einsum('bqhd,bqhkd->bqhk', q, k_sel).astype(jnp.float32) * scale
        logits.append(jnp.where(valid, s, NEG))
        parts.append(('sel', v_sel))
    for k_o, v_o, m in own_segs:
        s = jnp.einsum('bqhd,bkhd->bqhk', q, k_o).astype(jnp.float32) * scale
        logits.append(jnp.where(m[None, :, None, :], s, NEG))
        parts.append(('own', v_o))
    w = jax.nn.softmax(jnp.concatenate(logits, axis=-1), axis=-1)
    out = None
    start = 0
    for kind, v in parts:
        n = v.shape[-2] if kind == 'sel' else v.shape[1]
        wp = w[..., start:start + n].astype(v.dtype)
        start += n
        if kind == 'sel':
            o = jnp.einsum('bqhk,bqhkd->bqhd', wp, v)
        else:
            o = jnp.einsum('bqhk,bkhd->bqhd', wp, v)
        out = o if out is None else out + o
    return out


def moba_prompt(q, k, v):
    B, S, H, D = q.shape
    nb = -(-S // MOBA_BLOCK)
    pad = nb * MOBA_BLOCK - S
    kb = jnp.pad(k, ((0, 0), (0, pad), (0, 0), (0, 0))).reshape(B, nb, MOBA_BLOCK, H, D)
    vb = jnp.pad(v, ((0, 0), (0, pad), (0, 0), (0, 0))).reshape(B, nb, MOBA_BLOCK, H, D)
    n_cand = nb - 1
    n_sel = min(MOBA_TOPK, n_cand)
    k_mean = jnp.mean(kb[:, :n_cand].astype(jnp.float32), axis=2)

    def chunk(c):
        start = c * Q_BLOCK
        q_c = lax.dynamic_slice_in_dim(q, start, Q_BLOCK, axis=1)
        qpos = start + jnp.arange(Q_BLOCK)
        j = start // MOBA_BLOCK
        k_own = lax.dynamic_index_in_dim(kb, j, axis=1, keepdims=False)
        v_own = lax.dynamic_index_in_dim(vb, j, axis=1, keepdims=False)
        own_pos = j * MOBA_BLOCK + jnp.arange(MOBA_BLOCK)
        mask = own_pos[None, :] <= qpos[:, None]
        return moba_attend(q_c, j, k_mean, kb, vb, n_sel, [(k_own, v_own, mask)])

    out = lax.map(chunk, jnp.arange(S // Q_BLOCK))
    return out.transpose(1, 0, 2, 3, 4).reshape(B, S, H, D)


def diff_lambda_value(lam, layer):
    lam = lam.astype(jnp.float32)
    lam_init = 0.8 - 0.6 * math.exp(-0.3 * layer)
    val = jnp.exp(jnp.sum(lam[0] * lam[1])) - jnp.exp(jnp.sum(lam[2] * lam[3])) + lam_init
    return val, lam_init


def diff_attend(q, segs, lam, lam_init, subln_g):
    scale = DIFF_DIM ** -0.5
    s = [jnp.where(m[None, None, None],
                   jnp.einsum('bqhcd,bkhcd->bhcqk', q, k).astype(jnp.float32) * scale, NEG)
         for k, v, m in segs]
    pr = jax.nn.softmax(jnp.concatenate(s, axis=-1), axis=-1)
    a = pr[:, :, 0] - lam * pr[:, :, 1]
    out = None
    start = 0
    for k, v, m in segs:
        n = k.shape[1]
        o = jnp.einsum('bhqk,bkhd->bqhd', a[..., start:start + n].astype(v.dtype), v)
        start += n
        out = o if out is None else out + o
    return rmsnorm(out, subln_g) * (1.0 - lam_init)


def diff_prompt(q, k, v, lam, lam_init, subln_g):
    B, S = q.shape[:2]
    kpos = jnp.arange(S)

    def chunk(c):
        start = c * Q_BLOCK
        q_c = lax.dynamic_slice_in_dim(q, start, Q_BLOCK, axis=1)
        qpos = start + jnp.arange(Q_BLOCK)
        mask = kpos[None, :] <= qpos[:, None]
        return diff_attend(q_c, [(k, v, mask)], lam, lam_init, subln_g)

    out = lax.map(chunk, jnp.arange(S // Q_BLOCK))
    return out.transpose(1, 0, 2, 3, 4).reshape(B, S, DIFF_HEADS, DIFF_VDIM)


def mem_kv(mem, p):
    B, M, _ = mem.shape
    z = rmsnorm(mem, p['mem_norm']) @ p['w_mem_kv']
    k, v = jnp.split(z, 2, axis=-1)
    k = rmsnorm(k.reshape(B, M, MEM_HEADS, MEM_DIM), p['mem_kn'])
    return k, v.reshape(B, M, MEM_HEADS, MEM_DIM)


def mem_attend(q, mk, mv):
    s = jnp.einsum('bqhd,bmhd->bhqm', q, mk).astype(jnp.float32) * (MEM_DIM ** -0.5)
    w = jax.nn.softmax(s, axis=-1).astype(mv.dtype)
    return jnp.einsum('bhqm,bmhd->bqhd', w, mv)


def merge_and_ffn(x, h, o_moba, o_diff, o_mem, p):
    B, T, _ = x.shape
    o = jnp.stack([o_moba.reshape(B, T, BRANCH_W), o_diff.reshape(B, T, BRANCH_W),
                   o_mem.reshape(B, T, BRANCH_W)], axis=2)
    proj = jnp.einsum('btnc,ncd->btnd', o, p['w_branch'])
    g = jax.nn.sigmoid(h @ p['w_gate'] + p['b_gate']).reshape(B, T, N_BRANCH, D_MODEL)
    x = x + jnp.sum(g * proj, axis=2) @ p['w_out']
    u = jnp.square(jax.nn.relu(rmsnorm(x, p['norm_ffn']) @ p['w_up']))
    return x + u @ p['w_down']


def prompt_layer(x, mem, pos, p, lam, lam_init):
    B, S, _ = x.shape
    h = rmsnorm(x, p['norm_mix'])
    mq, mk, mv, dq, dk, dv, cq = project_heads(h, p, pos)
    o_moba = moba_prompt(mq, mk, mv)
    o_diff = diff_prompt(dq, dk, dv, lam, lam_init, p['diff_subln'])
    m_k, m_v = mem_kv(mem, p)
    o_mem = mem_attend(cq, m_k, m_v)
    y = merge_and_ffn(x, h, o_moba, o_diff, o_mem, p)
    return y, mk, mv, dk.reshape(B, S, DIFF_HEADS, 2 * DIFF_DIM), dv, m_k, m_v


def sample_layer(x, pos, past, l, c_mk, c_mv, c_dk, c_dv, c_ck, c_cv, page_table, p, lam, lam_init):
    B, T, _ = x.shape
    h = rmsnorm(x, p['norm_mix'])
    mq, mk, mv, dq, dk, dv, cq = project_heads(h, p, pos)
    tril = jnp.arange(T)[None, :] <= jnp.arange(T)[:, None]
    k_past = c_mk[l, page_table].reshape(B, past, MOBA_HEADS, HEAD_DIM)
    v_past = c_mv[l, page_table].reshape(B, past, MOBA_HEADS, HEAD_DIM)
    n_full = past // MOBA_BLOCK
    tail = past - n_full * MOBA_BLOCK
    kb = k_past[:, :n_full * MOBA_BLOCK].reshape(B, n_full, MOBA_BLOCK, MOBA_HEADS, HEAD_DIM)
    vb = v_past[:, :n_full * MOBA_BLOCK].reshape(B, n_full, MOBA_BLOCK, MOBA_HEADS, HEAD_DIM)
    k_mean = jnp.mean(kb.astype(jnp.float32), axis=2)
    own = []
    if tail > 0:
        own.append((k_past[:, n_full * MOBA_BLOCK:], v_past[:, n_full * MOBA_BLOCK:],
                    jnp.ones((T, tail), dtype=bool)))
    own.append((mk, mv, tril))
    o_moba = moba_attend(mq, n_full, k_mean, kb, vb, min(MOBA_TOPK, n_full), own)
    kd_past = c_dk[l, page_table].reshape(B, past, DIFF_HEADS, 2, DIFF_DIM)
    vd_past = c_dv[l, page_table].reshape(B, past, DIFF_HEADS, DIFF_VDIM)
    o_diff = diff_attend(dq, [(kd_past, vd_past, jnp.ones((T, past), dtype=bool)), (dk, dv, tril)],
                         lam, lam_init, p['diff_subln'])
    o_mem = mem_attend(cq, c_ck[l], c_cv[l])
    y = merge_and_ffn(x, h, o_moba, o_diff, o_mem, p)
    return y, mk, mv, dk.reshape(B, T, DIFF_HEADS, 2 * DIFF_DIM), dv


def setup_inputs(seed: int = 0) -> dict:
    key = jax.random.key(seed)
    ks = jax.random.split(key, 40)
    n_pages = PAST_LEN // PAGE_SIZE
    n_pool = (DEC_BATCH * n_pages * 5) // 4
    f32 = jnp.float32

    def nrm(k, shape, scale=1.0):
        return jax.random.normal(k, shape, dtype=f32) * scale

    def gain(k, shape):
        return 1.0 + 0.02 * jax.random.normal(k, shape, dtype=f32)

    perm = jax.random.permutation(ks[9], n_pool)
    page_table = perm[:DEC_BATCH * n_pages].reshape(DEC_BATCH, n_pages).astype(jnp.int32)
    return {
        'x_prompt': nrm(ks[0], (BATCH, SEQ, D_MODEL)),
        'x_sample': nrm(ks[1], (DEC_BATCH, DEC_SEQ, D_MODEL)),
        'cache_moba_k': nrm(ks[2], (DEPTH, n_pool, PAGE_SIZE, MOBA_HEADS, HEAD_DIM)),
        'cache_moba_v': nrm(ks[3], (DEPTH, n_pool, PAGE_SIZE, MOBA_HEADS, HEAD_DIM)),
        'cache_diff_k': nrm(ks[4], (DEPTH, n_pool, PAGE_SIZE, DIFF_HEADS, 2 * DIFF_DIM)),
        'cache_diff_v': nrm(ks[5], (DEPTH, n_pool, PAGE_SIZE, DIFF_HEADS, DIFF_VDIM)),
        'cache_mem_k': nrm(ks[6], (DEPTH, DEC_BATCH, MEM_LEN, MEM_HEADS, MEM_DIM)),
        'cache_mem_v': nrm(ks[7], (DEPTH, DEC_BATCH, MEM_LEN, MEM_HEADS, MEM_DIM)),
        'page_table': page_table,
        'mem_prompt': nrm(ks[8], (BATCH, MEM_LEN, D_MODEL)),
        'norm_mix': gain(ks[10], (DEPTH, D_MODEL)),
        'w_in': nrm(ks[11], (DEPTH, D_MODEL, D_IN), D_MODEL ** -0.5),
        'moba_qn': gain(ks[12], (DEPTH, HEAD_DIM)),
        'moba_kn': gain(ks[13], (DEPTH, HEAD_DIM)),
        'diff_qn': gain(ks[14], (DEPTH, DIFF_DIM)),
        'diff_kn': gain(ks[15], (DEPTH, DIFF_DIM)),
        'diff_lambda': nrm(ks[16], (DEPTH, 4, DIFF_DIM), 0.1),
        'diff_subln': gain(ks[17], (DEPTH, DIFF_VDIM)),
        'mem_norm': gain(ks[18], (DEPTH, D_MODEL)),
        'w_mem_kv': nrm(ks[19], (DEPTH, D_MODEL, 2 * MEM_W), D_MODEL ** -0.5),
        'mem_qn': gain(ks[20], (DEPTH, MEM_DIM)),
        'mem_kn': gain(ks[21], (DEPTH, MEM_DIM)),
        'w_branch': nrm(ks[22], (DEPTH, N_BRANCH, BRANCH_W, D_MODEL), BRANCH_W ** -0.5),
        'w_gate': nrm(ks[23], (DEPTH, D_MODEL, N_BRANCH * D_MODEL), D_MODEL ** -0.5),
        'b_gate': nrm(ks[24], (DEPTH, N_BRANCH * D_MODEL), 0.01),
        'w_out': nrm(ks[25], (DEPTH, D_MODEL, D_MODEL), D_MODEL ** -0.5),
        'norm_ffn': gain(ks[26], (DEPTH, D_MODEL)),
        'w_up': nrm(ks[27], (DEPTH, D_MODEL, D_FF), D_MODEL ** -0.5),
        'w_down': nrm(ks[28], (DEPTH, D_FF, D_MODEL), D_FF ** -0.5),
    }


def reference(x_prompt, x_sample, cache_moba_k, cache_moba_v, cache_diff_k, cache_diff_v,
              cache_mem_k, cache_mem_v, page_table, mem_prompt, norm_mix, w_in, moba_qn, moba_kn,
              diff_qn, diff_kn, diff_lambda, diff_subln, mem_norm, w_mem_kv, mem_qn, mem_kn,
              w_branch, w_gate, b_gate, w_out, norm_ffn, w_up, w_down):
    seq = x_prompt.shape[1]
    t_dec = x_sample.shape[1]
    past = page_table.shape[1] * cache_moba_k.shape[2]
    pos_p = jnp.arange(seq, dtype=jnp.int32)
    pos_s = past + jnp.arange(t_dec, dtype=jnp.int32)
    xp, xs = x_prompt, x_sample
    mkp, mvp, dkp, dvp, ckp, cvp = [], [], [], [], [], []
    mks, mvs, dks, dvs = [], [], [], []
    for l in range(DEPTH):
        p = {'norm_mix': norm_mix[l], 'w_in': w_in[l], 'moba_qn': moba_qn[l], 'moba_kn': moba_kn[l],
             'diff_qn': diff_qn[l], 'diff_kn': diff_kn[l], 'diff_subln': diff_subln[l],
             'mem_norm': mem_norm[l], 'w_mem_kv': w_mem_kv[l], 'mem_qn': mem_qn[l], 'mem_kn': mem_kn[l],
             'w_branch': w_branch[l], 'w_gate': w_gate[l], 'b_gate': b_gate[l], 'w_out': w_out[l],
             'norm_ffn': norm_ffn[l], 'w_up': w_up[l], 'w_down': w_down[l]}
        lam, lam_init = diff_lambda_value(diff_lambda[l], l)
        xp, a_k, a_v, b_k, b_v, c_k, c_v = prompt_layer(xp, mem_prompt, pos_p, p, lam, lam_init)
        mkp.append(a_k); mvp.append(a_v); dkp.append(b_k); dvp.append(b_v); ckp.append(c_k); cvp.append(c_v)
        xs, s_mk, s_mv, s_dk, s_dv = sample_layer(xs, pos_s, past, l, cache_moba_k, cache_moba_v,
                                                  cache_diff_k, cache_diff_v, cache_mem_k, cache_mem_v,
                                                  page_table, p, lam, lam_init)
        mks.append(s_mk); mvs.append(s_mv); dks.append(s_dk); dvs.append(s_dv)
    return (xp, xs, jnp.stack(mkp), jnp.stack(mvp), jnp.stack(dkp), jnp.stack(dvp),
            jnp.stack(ckp), jnp.stack(cvp), jnp.stack(mks), jnp.stack(mvs), jnp.stack(dks), jnp.stack(dvs))
```

```python
import functools
import math

import jax
import jax.numpy as jnp
from jax import lax
from jax.experimental import pallas as pl
from jax.experimental.pallas import tpu as pltpu

F32 = jnp.float32
BF16 = jnp.bfloat16

NEG = -1e30
NORM_EPS = 1e-6
ROPE_THETA = 10000.0
SEG = 512
HEAD = 64
MOBA_BLOCK = 256
MOBA_TOPK = 3
MEM_HEADS = 4
MEM_DIM = 128
DIFF_HEADS = 4
DIFF_VDIM = 128
T_PAD = 8
PAGES_PER_STEP = 16
VMEM_LIMIT = 56 * 1024 * 1024


def _cparams(sem):
    return pltpu.CompilerParams(dimension_semantics=sem, vmem_limit_bytes=VMEM_LIMIT)


def _rms(x, g):
    ms = jnp.mean(x * x, axis=-1, keepdims=True)
    return x * lax.rsqrt(ms + NORM_EPS) * g


def _dot(a, b):
    return jnp.dot(a, b, preferred_element_type=F32)


def _dot_nt(a, b):
    return lax.dot_general(a, b, (((1,), (1,)), ((), ())), preferred_element_type=F32)


def _split(a):
    hi = a.astype(BF16)
    lo = (a - hi.astype(F32)).astype(BF16)
    return hi, lo


def _dot_nt_split(a, b):
    ah, al = _split(a)
    bh, bl = _split(b)
    return _dot_nt(ah, bh) + _dot_nt(al, bh) + _dot_nt(ah, bl)


def _dot_ones_split(a, ones_bf16):
    ah, al = _split(a)
    return _dot(ah, ones_bf16) + _dot(al, ones_bf16)


def _top3_lanes(gate, valid):
    g = jnp.where(valid, gate, NEG)
    iota = lax.broadcasted_iota(jnp.int32, g.shape, 1).astype(F32)
    sel = jnp.zeros(g.shape, dtype=jnp.bool_)
    for _ in range(MOBA_TOPK):
        mx = jnp.max(g, axis=1, keepdims=True)
        idx = jnp.min(jnp.where(g == mx, iota, 1e9), axis=1, keepdims=True)
        pick = iota == idx
        sel = jnp.logical_or(sel, pick)
        g = jnp.where(pick, -jnp.inf, g)
    return jnp.logical_and(sel, valid)


def _proj_kernel(x_ref, nm_ref, w_ref, gains_ref, cos_ref, sin_ref, gmat_ref,
                 mq_ref, mk_ref, mv_ref, dq_ref, dk_ref, dv_ref, cq_ref):
    x = x_ref[...]
    h = _rms(x, nm_ref[...]).astype(BF16)
    cos = jnp.concatenate([cos_ref[...]] * 4, axis=1)
    sin = jnp.concatenate([sin_ref[...]] * 4, axis=1)
    lane = lax.broadcasted_iota(jnp.int32, (1, SEG), 1)
    first_half = (lane % HEAD) < (HEAD // 2)

    def seg(i):
        return _dot(h, w_ref[:, i * SEG:(i + 1) * SEG])

    def norm_rope(z, gain):
        ms = _dot(z * z, gmat_ref[...])
        zn = z * lax.rsqrt(ms + NORM_EPS) * gain
        swapped = jnp.where(first_half, pltpu.roll(zn, SEG - HEAD // 2, 1),
                            pltpu.roll(zn, HEAD // 2, 1))
        return zn * cos + swapped * sin

    mq_ref[...] = norm_rope(seg(0), gains_ref[0:1, :])
    mk_ref[...] = norm_rope(seg(1), gains_ref[1:2, :])
    mv_ref[...] = seg(2)
    dq_ref[...] = norm_rope(seg(3), gains_ref[2:3, :])
    dk_ref[...] = norm_rope(seg(4), gains_ref[3:4, :])
    dv_ref[...] = seg(5)
    zc = seg(6)
    for hh in range(MEM_HEADS):
        sl = slice(hh * MEM_DIM, (hh + 1) * MEM_DIM)
        cq_ref[:, sl] = _rms(zc[:, sl], gains_ref[4:5, sl])


def _project(x2d, norm_mix, w_in_bf16, gains, cos_t, sin_t, gmat, tm, n_pos_tiles):
    n, d = x2d.shape
    d_in = w_in_bf16.shape[1]
    full = lambda i: (0, 0)
    tok = lambda i: (i, 0)
    pos = lambda i: (i % n_pos_tiles, 0)
    out = jax.ShapeDtypeStruct((n, SEG), F32)
    return pl.pallas_call(
        _proj_kernel,
        grid=(n // tm,),
        in_specs=[pl.BlockSpec((tm, d), tok),
                  pl.BlockSpec((1, d), full),
                  pl.BlockSpec((d, d_in), full),
                  pl.BlockSpec((8, SEG), full),
                  pl.BlockSpec((tm, 128), pos),
                  pl.BlockSpec((tm, 128), pos),
                  pl.BlockSpec((SEG, SEG), full)],
        out_specs=[pl.BlockSpec((tm, SEG), tok)] * 7,
        out_shape=[out] * 7,
        compiler_params=_cparams(("parallel",)),
        name="proj",
    )(x2d, norm_mix, w_in_bf16, gains, cos_t, sin_t, gmat)


def _memkv_kernel(mem_ref, mn_ref, w_ref, kn_ref, k_ref, v_ref):
    m = _rms(mem_ref[0], mn_ref[...]).astype(BF16)
    z = _dot(m, w_ref[...])
    for hh in range(MEM_HEADS):
        sl = slice(hh * MEM_DIM, (hh + 1) * MEM_DIM)
        k_ref[0, :, sl] = _rms(z[:, sl], kn_ref[...])
    v_ref[0] = z[:, SEG:]


def _mem_kv(mem, mem_norm, w_mem_kv_bf16, mem_kn):
    b, m, d = mem.shape
    out = jax.ShapeDtypeStruct((b, m, SEG), F32)
    return pl.pallas_call(
        _memkv_kernel,
        grid=(b,),
        in_specs=[pl.BlockSpec((1, m, d), lambda i: (i, 0, 0)),
                  pl.BlockSpec((1, d), lambda i: (0, 0)),
                  pl.BlockSpec((d, 2 * SEG), lambda i: (0, 0)),
                  pl.BlockSpec((1, MEM_DIM), lambda i: (0, 0))],
        out_specs=[pl.BlockSpec((1, m, SEG), lambda i: (i, 0, 0))] * 2,
        out_shape=[out, out],
        compiler_params=_cparams(("parallel",)),
        name="mem_kv",
    )(mem, mem_norm, w_mem_kv_bf16, mem_kn)


def _mem_attn_kernel(q_ref, k_ref, v_ref, o_ref):
    scale = MEM_DIM ** -0.5
    for hh in range(MEM_HEADS):
        sl = slice(hh * MEM_DIM, (hh + 1) * MEM_DIM)
        s = _dot_nt(q_ref[0, :, sl], k_ref[0, :, sl]) * scale
        p = jnp.exp(s - jnp.max(s, axis=1, keepdims=True))
        o_ref[0, :, sl] = _dot(p, v_ref[0, :, sl]) / jnp.sum(p, axis=1, keepdims=True)


def _mem_attend(q, k, v, tq):
    b, t, _ = q.shape
    m = k.shape[1]
    return pl.pallas_call(
        _mem_attn_kernel,
        grid=(b, t // tq),
        in_specs=[pl.BlockSpec((1, tq, SEG), lambda i, j: (i, j, 0)),
                  pl.BlockSpec((1, m, SEG), lambda i, j: (i, 0, 0)),
                  pl.BlockSpec((1, m, SEG), lambda i, j: (i, 0, 0))],
        out_specs=pl.BlockSpec((1, tq, SEG), lambda i, j: (i, j, 0)),
        out_shape=jax.ShapeDtypeStruct((b, t, SEG), F32),
        compiler_params=_cparams(("parallel", "parallel")),
        name="mem_attn",
    )(q, k, v)


def _moba_prompt_kernel(q_ref, k_ref, v_ref, o_ref, kmean_ref, m_ref, l_ref, acc_ref, *, nb):
    j = pl.program_id(2)
    blk = MOBA_BLOCK
    gw = 4 * HEAD
    nh = 4

    @pl.when(j == 0)
    def _():
        for n in range(nb):
            kmean_ref[n:n + 1, :] = jnp.mean(k_ref[0, n * blk:(n + 1) * blk, :], axis=0, keepdims=True)

    q = q_ref[0]
    lane = lax.broadcasted_iota(jnp.int32, (1, gw), 1)
    hmask = [(lane // HEAD) == hh for hh in range(nh)]
    qh = [jnp.where(hmask[hh], q, 0.0) for hh in range(nh)]

    blk_iota = lax.broadcasted_iota(jnp.int32, (blk, nb), 1)
    valid = blk_iota < j
    sel = []
    for hh in range(nh):
        gate = _dot_nt_split(qh[hh], kmean_ref[...])
        sel.append(_top3_lanes(gate, valid).astype(F32))
    qs = [qh[hh] * (HEAD ** -0.5) for hh in range(nh)]

    def attend(kb, vb, keep_fn, first):
        alphas = []
        pvs = []
        for hh in range(nh):
            s = _dot_nt(qs[hh], kb)
            s = jnp.where(keep_fn(hh), s, NEG)
            m_old = m_ref[hh]
            m_new = jnp.maximum(m_old, jnp.max(s, axis=1, keepdims=True))
            alpha = jnp.exp(m_old - m_new)
            p = jnp.exp(s - m_new)
            l_ref[hh] = alpha * l_ref[hh] + jnp.sum(p, axis=1, keepdims=True)
            m_ref[hh] = m_new
            alphas.append(alpha)
            pvs.append(_dot(p, vb))
        acc = acc_ref[...]
        new = jnp.zeros_like(acc)
        scale = jnp.zeros_like(acc)
        for hh in range(nh):
            new = jnp.where(hmask[hh], pvs[hh], new)
            scale = jnp.where(hmask[hh], alphas[hh], scale)
        acc_ref[...] = new if first else acc * scale + new

    for hh in range(nh):
        m_ref[hh] = jnp.full((blk, 1), NEG, F32)
        l_ref[hh] = jnp.zeros((blk, 1), F32)
    row = lax.broadcasted_iota(jnp.int32, (blk, blk), 0)
    col = lax.broadcasted_iota(jnp.int32, (blk, blk), 1)
    causal = col <= row
    own = pl.ds(pl.multiple_of(j * blk, blk), blk)
    attend(k_ref[0, own, :], v_ref[0, own, :], lambda hh: causal, True)

    def body(n, carry):
        pick = lax.broadcasted_iota(jnp.int32, (blk, nb), 1) == n
        rows = pl.ds(pl.multiple_of(n * blk, blk), blk)
        keep = [jnp.max(jnp.where(pick, sel[hh], 0.0), axis=1, keepdims=True) > 0.5 for hh in range(nh)]
        attend(k_ref[0, rows, :], v_ref[0, rows, :], lambda hh: keep[hh], False)
        return carry

    lax.fori_loop(0, j, body, 0)

    inv = jnp.zeros((blk, gw), F32)
    for hh in range(nh):
        inv = jnp.where(hmask[hh], 1.0 / l_ref[hh], inv)
    o_ref[0] = acc_ref[...] * inv


def _moba_prompt(q, k, v):
    b, s, _ = q.shape
    nb = s // MOBA_BLOCK
    gw = 4 * HEAD
    ng = SEG // gw
    return pl.pallas_call(
        functools.partial(_moba_prompt_kernel, nb=nb),
        grid=(b, ng, nb),
        in_specs=[pl.BlockSpec((1, MOBA_BLOCK, gw), lambda i, g, j: (i, j, g)),
                  pl.BlockSpec((1, s, gw), lambda i, g, j: (i, 0, g)),
                  pl.BlockSpec((1, s, gw), lambda i, g, j: (i, 0, g))],
        out_specs=pl.BlockSpec((1, MOBA_BLOCK, gw), lambda i, g, j: (i, j, g)),
        out_shape=jax.ShapeDtypeStruct((b, s, SEG), F32),
        scratch_shapes=[pltpu.VMEM((nb, gw), F32),
                        pltpu.VMEM((4, MOBA_BLOCK, 1), F32),
                        pltpu.VMEM((4, MOBA_BLOCK, 1), F32),
                        pltpu.VMEM((MOBA_BLOCK, gw), F32)],
        compiler_params=_cparams(("parallel", "parallel", "arbitrary")),
        name="moba_prompt",
    )(q, k, v)


def _diff_lambda(lam_ref, lam_init):
    lam = lam_ref[...]
    a = jnp.sum(lam[0:1, :] * lam[1:2, :], axis=1, keepdims=True)
    b = jnp.sum(lam[2:3, :] * lam[3:4, :], axis=1, keepdims=True)
    return jnp.exp(a) - jnp.exp(b) + lam_init


def _diff_prompt_kernel(q_ref, k_ref, v_ref, lam_ref, g_ref, o_ref, m_ref, l_ref, acc_ref, *, tq, lam_init):
    i = pl.program_id(2)
    tk = tq
    q = q_ref[0] * (HEAD ** -0.5)
    lane = lax.broadcasted_iota(jnp.int32, (1, 2 * HEAD), 1)
    qs = jnp.concatenate([jnp.where(lane < HEAD, q, 0.0), jnp.where(lane >= HEAD, q, 0.0)], axis=0)

    m_ref[...] = jnp.full((2 * tq, 1), NEG, F32)
    l_ref[...] = jnp.zeros((2 * tq, 1), F32)
    acc_ref[...] = jnp.zeros((2 * tq, DIFF_VDIM), F32)

    def attend(rows, mask):
        s = _dot_nt(qs, k_ref[0, rows, :])
        if mask is not None:
            s = jnp.where(mask, s, NEG)
        m_old = m_ref[...]
        m_new = jnp.maximum(m_old, jnp.max(s, axis=1, keepdims=True))
        alpha = jnp.exp(m_old - m_new)
        p = jnp.exp(s - m_new)
        l_ref[...] = alpha * l_ref[...] + jnp.sum(p, axis=1, keepdims=True)
        m_ref[...] = m_new
        acc_ref[...] = alpha * acc_ref[...] + _dot(p, v_ref[0, rows, :])

    def body(n, carry):
        attend(pl.ds(pl.multiple_of(n * tk, tk), tk), None)
        return carry

    lax.fori_loop(0, i, body, 0)
    row = lax.broadcasted_iota(jnp.int32, (2 * tq, tk), 0) % tq
    col = lax.broadcasted_iota(jnp.int32, (2 * tq, tk), 1)
    attend(pl.ds(pl.multiple_of(i * tk, tk), tk), col <= row)

    o = acc_ref[...] / l_ref[...]
    lam = _diff_lambda(lam_ref, lam_init)
    d = o[:tq] - lam * o[tq:]
    o_ref[0] = _rms(d, g_ref[...]) * (1.0 - lam_init)


def _diff_prompt(q, k, v, lam, subln, lam_init, tq):
    b, s, _ = q.shape
    hw = 2 * HEAD
    return pl.pallas_call(
        functools.partial(_diff_prompt_kernel, tq=tq, lam_init=lam_init),
        grid=(b, DIFF_HEADS, s // tq),
        in_specs=[pl.BlockSpec((1, tq, hw), lambda i, h, j: (i, j, h)),
                  pl.BlockSpec((1, s, hw), lambda i, h, j: (i, 0, h)),
                  pl.BlockSpec((1, s, DIFF_VDIM), lambda i, h, j: (i, 0, h)),
                  pl.BlockSpec((4, HEAD), lambda i, h, j: (0, 0)),
                  pl.BlockSpec((1, DIFF_VDIM), lambda i, h, j: (0, 0))],
        out_specs=pl.BlockSpec((1, tq, DIFF_VDIM), lambda i, h, j: (i, j, h)),
        out_shape=jax.ShapeDtypeStruct((b, s, SEG), F32),
        scratch_shapes=[pltpu.VMEM((2 * tq, 1), F32),
                        pltpu.VMEM((2 * tq, 1), F32),
                        pltpu.VMEM((2 * tq, DIFF_VDIM), F32)],
        compiler_params=_cparams(("parallel", "parallel", "arbitrary")),
        name="diff_prompt",
    )(q, k, v, lam, subln)


def _merge_kernel(x_ref, om_ref, od_ref, oc_ref, nm_ref, wg_ref, bg_ref, wb_ref, wo_ref, y_ref):
    x = x_ref[...]
    d = x.shape[1]
    h = _rms(x, nm_ref[...]).astype(BF16)
    merged = jnp.zeros_like(x)
    for n, o_ref in enumerate((om_ref, od_ref, oc_ref)):
        sl = slice(n * d, (n + 1) * d)
        g = jax.nn.sigmoid(_dot(h, wg_ref[:, sl]) + bg_ref[:, sl])
        merged = merged + g * _dot(o_ref[...].astype(BF16), wb_ref[n])
    y_ref[...] = x + _dot(merged.astype(BF16), wo_ref[...])


def _merge(x2d, o_moba, o_diff, o_mem, norm_mix, w_gate, b_gate, w_branch, w_out, tm):
    n, d = x2d.shape
    tok = lambda i: (i, 0)
    full = lambda i: (0, 0)
    return pl.pallas_call(
        _merge_kernel,
        grid=(n // tm,),
        in_specs=[pl.BlockSpec((tm, d), tok),
                  pl.BlockSpec((tm, SEG), tok),
                  pl.BlockSpec((tm, SEG), tok),
                  pl.BlockSpec((tm, SEG), tok),
                  pl.BlockSpec((1, d), full),
                  pl.BlockSpec((d, 3 * d), full),
                  pl.BlockSpec((1, 3 * d), full),
                  pl.BlockSpec((3, SEG, d), lambda i: (0, 0, 0)),
                  pl.BlockSpec((d, d), full)],
        out_specs=pl.BlockSpec((tm, d), tok),
        out_shape=jax.ShapeDtypeStruct((n, d), F32),
        compiler_params=_cparams(("parallel",)),
        name="merge",
    )(x2d, o_moba, o_diff, o_mem, norm_mix, w_gate, b_gate, w_branch, w_out)


def _ffn_kernel(x_ref, nf_ref, wu_ref, wd_ref, y_ref, *, n_chunks):
    x = x_ref[...]
    hn = _rms(x, nf_ref[...]).astype(BF16)
    cw = wu_ref.shape[1] // n_chunks
    acc = x
    for c in range(n_chunks):
        u = jnp.square(jnp.maximum(_dot(hn, wu_ref[:, c * cw:(c + 1) * cw]), 0.0))
        acc = acc + _dot(u.astype(BF16), wd_ref[c * cw:(c + 1) * cw, :])
    y_ref[...] = acc


def _ffn(x2d, norm_ffn, w_up, w_down, tm):
    n, d = x2d.shape
    dff = w_up.shape[1]
    tok = lambda i: (i, 0)
    full = lambda i: (0, 0)
    return pl.pallas_call(
        functools.partial(_ffn_kernel, n_chunks=4),
        grid=(n // tm,),
        in_specs=[pl.BlockSpec((tm, d), tok),
                  pl.BlockSpec((1, d), full),
                  pl.BlockSpec((d, dff), full),
                  pl.BlockSpec((dff, d), full)],
        out_specs=pl.BlockSpec((tm, d), tok),
        out_shape=jax.ShapeDtypeStruct((n, d), F32),
        compiler_params=_cparams(("parallel",)),
        name="ffn",
    )(x2d, norm_ffn, w_up, w_down)


def _group_query(q8):
    rows = lax.broadcasted_iota(jnp.int32, (8 * T_PAD, SEG), 0) // T_PAD
    lanes = lax.broadcasted_iota(jnp.int32, (8 * T_PAD, SEG), 1) // HEAD
    return jnp.where(rows == lanes, jnp.concatenate([q8] * 8, axis=0), 0.0)


def _compact(x, group_lanes, groups):
    lane = lax.broadcasted_iota(jnp.int32, (T_PAD, SEG), 1) // group_lanes
    out = jnp.zeros((T_PAD, SEG), F32)
    for r, g in groups:
        piece = x[r * T_PAD:(r + 1) * T_PAD, :]
        out = jnp.where(lane == g, jnp.broadcast_to(piece, (T_PAD, SEG)), out)
    return out


def _diff_sample_kernel(pt_ref, q_ref, kn_ref, vn_ref, lam_ref, g_ref, *rest, npg, lam_init, t_dec):
    del pt_ref
    k_refs, v_refs = rest[:npg], rest[npg:2 * npg]
    o_ref, m_ref, l_ref, acc_ref = rest[2 * npg:]
    c = pl.program_id(1)
    qg = _group_query(q_ref[0] * (HEAD ** -0.5))

    @pl.when(c == 0)
    def _():
        m_ref[...] = jnp.full(m_ref.shape, NEG, F32)
        l_ref[...] = jnp.zeros(l_ref.shape, F32)
        acc_ref[...] = jnp.zeros(acc_ref.shape, F32)

    s = jnp.concatenate([_dot_nt(qg, k_refs[p][...]) for p in range(npg)], axis=1)
    m_old = m_ref[...]
    m_new = jnp.maximum(m_old, jnp.max(s, axis=1, keepdims=True))
    alpha = jnp.exp(m_old - m_new)
    p_all = jnp.exp(s - m_new)
    pg = k_refs[0].shape[0]
    pv = _dot(p_all[:, :pg], v_refs[0][...])
    for p in range(1, npg):
        pv = pv + _dot(p_all[:, p * pg:(p + 1) * pg], v_refs[p][...])
    l_ref[...] = alpha * l_ref[...] + jnp.sum(p_all, axis=1, keepdims=True)
    acc_ref[...] = alpha * acc_ref[...] + pv
    m_ref[...] = m_new

    @pl.when(c == pl.num_programs(1) - 1)
    def _():
        kn = kn_ref[0]
        vn = vn_ref[0]
        tok = lax.broadcasted_iota(jnp.int32, (8 * T_PAD, 1), 0) % T_PAD
        sn = [jnp.where(tok >= t, jnp.sum(qg * kn[t:t + 1, :], axis=1, keepdims=True), NEG) for t in range(t_dec)]
        m0 = m_ref[...]
        m1 = m0
        for t in range(t_dec):
            m1 = jnp.maximum(m1, sn[t])
        a = jnp.exp(m0 - m1)
        l = a * l_ref[...]
        acc = a * acc_ref[...]
        for t in range(t_dec):
            pt = jnp.exp(sn[t] - m1)
            l = l + pt
            acc = acc + pt * vn[t:t + 1, :]
        o = acc / l
        o1 = _compact(o, DIFF_VDIM, [(2 * hh, hh) for hh in range(DIFF_HEADS)])
        o2 = _compact(o, DIFF_VDIM, [(2 * hh + 1, hh) for hh in range(DIFF_HEADS)])
        d = o1 - _diff_lambda(lam_ref, lam_init) * o2
        for hh in range(DIFF_HEADS):
            sl = slice(hh * DIFF_VDIM, (hh + 1) * DIFF_VDIM)
            o_ref[0, :, sl] = _rms(d[:, sl], g_ref[...]) * (1.0 - lam_init)


def _page_specs(layer, npg, page, width):
    def spec(jj):
        return pl.BlockSpec((None, None, page, width),
                            lambda b, c, pt: (layer, pt[b, c * npg + jj], 0, 0))
    return [spec(jj) for jj in range(npg)]


def _diff_sample(page_table, q, kn, vn, lam, subln, cache_k, cache_v, layer, lam_init, t_dec):
    bd = q.shape[0]
    n_pages = page_table.shape[1]
    page = cache_k.shape[2]
    npg = min(PAGES_PER_STEP, n_pages)
    tokspec = pl.BlockSpec((1, T_PAD, SEG), lambda b, c, pt: (b, 0, 0))
    grid_spec = pltpu.PrefetchScalarGridSpec(
        num_scalar_prefetch=1,
        grid=(bd, n_pages // npg),
        in_specs=[tokspec, tokspec, tokspec,
                  pl.BlockSpec((4, HEAD), lambda b, c, pt: (0, 0)),
                  pl.BlockSpec((1, DIFF_VDIM), lambda b, c, pt: (0, 0))]
                 + _page_specs(layer, npg, page, SEG) + _page_specs(layer, npg, page, SEG),
        out_specs=tokspec,
        scratch_shapes=[pltpu.VMEM((8 * T_PAD, 1), F32),
                        pltpu.VMEM((8 * T_PAD, 1), F32),
                        pltpu.VMEM((8 * T_PAD, SEG), F32)])
    return pl.pallas_call(
        functools.partial(_diff_sample_kernel, npg=npg, lam_init=lam_init, t_dec=t_dec),
        grid_spec=grid_spec,
        out_shape=jax.ShapeDtypeStruct((bd, T_PAD, SEG), F32),
        compiler_params=_cparams(("parallel", "arbitrary")),
        name="diff_sample",
    )(page_table, q, kn, vn, lam, subln, *([cache_k] * npg), *([cache_v] * npg))


def _moba_sample_kernel(pt_ref, q_ref, kn_ref, vn_ref, ones_ref, *rest, npg, t_dec):
    del pt_ref
    k_refs, v_refs = rest[:npg], rest[npg:2 * npg]
    o_ref, km_ref, pm_ref, pl_ref, po_ref = rest[2 * npg:]
    c = pl.program_id(1)
    q8 = q_ref[0]
    qg = _group_query(q8 * (HEAD ** -0.5))
    heads = [(hh, hh) for hh in range(8)]
    pages_per_block = MOBA_BLOCK // k_refs[0].shape[0]
    bpc = npg // pages_per_block
    pg = k_refs[0].shape[0]

    for bb in range(bpc):
        ks = [k_refs[bb * pages_per_block + p][...] for p in range(pages_per_block)]
        vs = [v_refs[bb * pages_per_block + p][...] for p in range(pages_per_block)]
        s = jnp.concatenate([_dot_nt(qg, kk) for kk in ks], axis=1)
        m = jnp.max(s, axis=1, keepdims=True)
        p = jnp.exp(s - m)
        l = jnp.sum(p, axis=1, keepdims=True)
        o = _dot(p[:, :pg], vs[0])
        for i in range(1, pages_per_block):
            o = o + _dot(p[:, i * pg:(i + 1) * pg], vs[i])
        ksum = jnp.sum(ks[0], axis=0, keepdims=True)
        for i in range(1, pages_per_block):
            ksum = ksum + jnp.sum(ks[i], axis=0, keepdims=True)
        n = c * bpc + bb
        km_ref[n] = jnp.broadcast_to(ksum * (1.0 / MOBA_BLOCK), (T_PAD, SEG))
        pm_ref[n] = _compact(m, HEAD, heads)
        pl_ref[n] = _compact(l, HEAD, heads)
        po_ref[n] = _compact(o, HEAD, heads)

    @pl.when(c == pl.num_programs(1) - 1)
    def _():
        nblk = km_ref.shape[0]
        ones = ones_ref[...]
        prod = (q8[None, :, :] * km_ref[...]).reshape(nblk * T_PAD, SEG)
        gate = _dot_ones_split(prod, ones).reshape(nblk, T_PAD, SEG)
        iota = lax.broadcasted_iota(jnp.int32, gate.shape, 0).astype(F32)
        g = gate
        sel = jnp.zeros(gate.shape, dtype=jnp.bool_)
        for _ in range(min(MOBA_TOPK, nblk)):
            mx = jnp.max(g, axis=0, keepdims=True)
            idx = jnp.min(jnp.where(g == mx, iota, 1e9), axis=0, keepdims=True)
            pick = iota == idx
            sel = jnp.logical_or(sel, pick)
            g = jnp.where(pick, -jnp.inf, g)

        kn = kn_ref[0]
        vn = vn_ref[0]
        qs = q8 * (HEAD ** -0.5)
        prod_n = jnp.concatenate([qs * kn[t:t + 1, :] for t in range(t_dec)], axis=0)
        s_new = _dot_ones_split(prod_n, ones)
        tok = lax.broadcasted_iota(jnp.int32, (T_PAD, SEG), 0)
        sn = [jnp.where(tok >= t, s_new[t * T_PAD:(t + 1) * T_PAD, :], NEG) for t in range(t_dec)]

        pm = jnp.where(sel, pm_ref[...], NEG)
        mt = jnp.max(pm, axis=0)
        for t in range(t_dec):
            mt = jnp.maximum(mt, sn[t])
        w = jnp.where(sel, jnp.exp(pm - mt[None]), 0.0)
        l = jnp.sum(w * pl_ref[...], axis=0)
        o = jnp.sum(w * po_ref[...], axis=0)
        for t in range(t_dec):
            pt = jnp.exp(sn[t] - mt)
            l = l + pt
            o = o + pt * vn[t:t + 1, :]
        o_ref[0] = o / l


def _moba_sample(page_table, q, kn, vn, ones, cache_k, cache_v, layer, t_dec):
    bd = q.shape[0]
    n_pages = page_table.shape[1]
    page = cache_k.shape[2]
    npg = min(PAGES_PER_STEP, n_pages)
    nblk = n_pages * page // MOBA_BLOCK
    tokspec = pl.BlockSpec((1, T_PAD, SEG), lambda b, c, pt: (b, 0, 0))
    part = pltpu.VMEM((nblk, T_PAD, SEG), F32)
    grid_spec = pltpu.PrefetchScalarGridSpec(
        num_scalar_prefetch=1,
        grid=(bd, n_pages // npg),
        in_specs=[tokspec, tokspec, tokspec,
                  pl.BlockSpec((SEG, SEG), lambda b, c, pt: (0, 0))]
                 + _page_specs(layer, npg, page, SEG) + _page_specs(layer, npg, page, SEG),
        out_specs=tokspec,
        scratch_shapes=[part, part, part, part])
    return pl.pallas_call(
        functools.partial(_moba_sample_kernel, npg=npg, t_dec=t_dec),
        grid_spec=grid_spec,
        out_shape=jax.ShapeDtypeStruct((bd, T_PAD, SEG), F32),
        compiler_params=_cparams(("parallel", "arbitrary")),
        name="moba_sample",
    )(page_table, q, kn, vn, ones, *([cache_k] * npg), *([cache_v] * npg))


def _rope_tables(pos):
    half = HEAD // 2
    inv = ROPE_THETA ** (-2.0 * jnp.arange(half, dtype=F32) / HEAD)
    ang = pos.astype(F32)[:, None] * inv[None, :]
    cos, sin = jnp.cos(ang), jnp.sin(ang)
    return jnp.concatenate([cos] * 4, axis=1), jnp.concatenate([-sin, sin] * 2, axis=1)


def _block_diag(width, group, value, dtype):
    i = jnp.arange(width) // group
    return jnp.where(i[:, None] == i[None, :], value, 0.0).astype(dtype)


def kernel(x_prompt, x_sample, cache_moba_k, cache_moba_v, cache_diff_k, cache_diff_v, cache_mem_k, cache_mem_v, page_table, mem_prompt, norm_mix, w_in, moba_qn, moba_kn, diff_qn, diff_kn, diff_lambda, diff_subln, mem_norm, w_mem_kv, mem_qn, mem_kn, w_branch, w_gate, b_gate, w_out, norm_ffn, w_up, w_down):
    depth = w_in.shape[0]
    b, s, d = x_prompt.shape
    bd, t_dec, _ = x_sample.shape
    n_pool, page = cache_moba_k.shape[1], cache_moba_k.shape[2]
    n_pages = page_table.shape[1]
    past = n_pages * page
    assert d == 2 * SEG and w_in.shape[2] == 7 * SEG
    assert s % 512 == 0 and past % MOBA_BLOCK == 0 and MOBA_BLOCK % page == 0
    assert t_dec <= T_PAD and n_pages % min(PAGES_PER_STEP, n_pages) == 0
    assert past // MOBA_BLOCK >= MOBA_TOPK

    tm = 512
    cos_p, sin_p = _rope_tables(jnp.arange(s, dtype=jnp.int32))
    cos_s, sin_s = _rope_tables(past + jnp.arange(T_PAD, dtype=jnp.int32))
    cos_s, sin_s = jnp.tile(cos_s, (bd, 1)), jnp.tile(sin_s, (bd, 1))
    gmat = _block_diag(SEG, HEAD, 1.0 / HEAD, F32)
    ones = _block_diag(SEG, HEAD, 1.0, BF16)

    w_in_b, w_mem_b = w_in.astype(BF16), w_mem_kv.astype(BF16)
    w_gate_b, w_branch_b, w_out_b = w_gate.astype(BF16), w_branch.astype(BF16), w_out.astype(BF16)
    w_up_b, w_down_b = w_up.astype(BF16), w_down.astype(BF16)

    c_mk = cache_moba_k.reshape(depth, n_pool, page, SEG)
    c_mv = cache_moba_v.reshape(depth, n_pool, page, SEG)
    c_dk = cache_diff_k.reshape(depth, n_pool, page, SEG)
    c_dv = cache_diff_v.reshape(depth, n_pool, page, SEG)
    c_ck = cache_mem_k.reshape(depth, bd, -1, SEG)
    c_cv = cache_mem_v.reshape(depth, bd, -1, SEG)

    xp = x_prompt.reshape(b * s, d)
    xs = jnp.pad(x_sample, ((0, 0), (0, T_PAD - t_dec), (0, 0))).reshape(bd * T_PAD, d)
    ns = bd * T_PAD
    outs = [[] for _ in range(10)]
    for l in range(depth):
        lam_init = 0.8 - 0.6 * math.exp(-0.3 * l)
        gains = jnp.concatenate([jnp.tile(moba_qn[l], 8)[None], jnp.tile(moba_kn[l], 8)[None],
                                 jnp.tile(diff_qn[l], 8)[None], jnp.tile(diff_kn[l], 8)[None],
                                 jnp.tile(mem_qn[l], 4)[None], jnp.zeros((3, SEG), F32)], axis=0)
        nm, nf = norm_mix[l][None], norm_ffn[l][None]
        subln = diff_subln[l][None]

        mq, mk, mv, dq, dk, dv, cq = _project(xp, nm, w_in_b[l], gains, cos_p, sin_p, gmat, tm, s // tm)
        r3 = lambda a: a.reshape(b, s, SEG)
        o_moba = _moba_prompt(r3(mq), r3(mk), r3(mv))
        o_diff = _diff_prompt(r3(dq), r3(dk), r3(dv), diff_lambda[l], subln, lam_init, 512)
        m_k, m_v = _mem_kv(mem_prompt, mem_norm[l][None], w_mem_b[l], mem_kn[l][None])
        o_mem = _mem_attend(r3(cq), m_k, m_v, 1024 if s % 1024 == 0 else 512)
        xp = _merge(xp, o_moba.reshape(b * s, SEG), o_diff.reshape(b * s, SEG), o_mem.reshape(b * s, SEG),
                    nm, w_gate_b[l], b_gate[l][None], w_branch_b[l], w_out_b[l], tm)
        xp = _ffn(xp, nf, w_up_b[l], w_down_b[l], tm)
        for lst, a in zip(outs[:6], (mk, mv, dk, dv, m_k, m_v)):
            lst.append(a)

        smq, smk, smv, sdq, sdk, sdv, scq = _project(xs, nm, w_in_b[l], gains, cos_s, sin_s, gmat, ns, 1)
        t3 = lambda a: a.reshape(bd, T_PAD, SEG)
        so_moba = _moba_sample(page_table, t3(smq), t3(smk), t3(smv), ones, c_mk, c_mv, l, t_dec)
        so_diff = _diff_sample(page_table, t3(sdq), t3(sdk), t3(sdv), diff_lambda[l], subln,
                               c_dk, c_dv, l, lam_init, t_dec)
        so_mem = _mem_attend(t3(scq), c_ck[l], c_cv[l], T_PAD)
        xs = _merge(xs, so_moba.reshape(ns, SEG), so_diff.reshape(ns, SEG), so_mem.reshape(ns, SEG),
                    nm, w_gate_b[l], b_gate[l][None], w_branch_b[l], w_out_b[l], ns)
        xs = _ffn(xs, nf, w_up_b[l], w_down_b[l], ns)
        for lst, a in zip(outs[6:], (smk, smv, sdk, sdv)):
            lst.append(t3(a)[:, :t_dec])

    st = lambda lst, shape: jnp.stack(lst).reshape((depth,) + shape)
    return (xp.reshape(b, s, d),
            xs.reshape(bd, T_PAD, d)[:, :t_dec],
            st(outs[0], (b, s, 8, HEAD)), st(outs[1], (b, s, 8, HEAD)),
            st(outs[2], (b, s, DIFF_HEADS, 2 * HEAD)), st(outs[3], (b, s, DIFF_HEADS, DIFF_VDIM)),
            st(outs[4], (b, -1, MEM_HEADS, MEM_DIM)), st(outs[5], (b, -1, MEM_HEADS, MEM_DIM)),
            st(outs[6], (bd, t_dec, 8, HEAD)), st(outs[7], (bd, t_dec, 8, HEAD)),
            st(outs[8], (bd, t_dec, DIFF_HEADS, 2 * HEAD)), st(outs[9], (bd, t_dec, DIFF_HEADS, DIFF_VDIM)))
```

```python
import functools
import math

import jax
import jax.numpy as jnp
from jax import lax
from jax.experimental import pallas as pl
from jax.experimental.pallas import tpu as pltpu

F32 = jnp.float32
BF16 = jnp.bfloat16

NEG = -1e30
NORM_EPS = 1e-6
ROPE_THETA = 10000.0
SEG = 512
HEAD = 64
MOBA_HEADS = 8
MOBA_BLOCK = 256
MOBA_TOPK = 3
MEM_HEADS = 4
MEM_DIM = 128
DIFF_HEADS = 4
DIFF_VDIM = 128
T_PAD = 8
PAGES_PER_STEP = 16
TOKEN_TILE = 512
QUERY_TILE = 256
KV_GROUP = 4
DIFF_KEY_TILE = 512
VMEM_LIMIT = 56 * 1024 * 1024


def _cparams(sem):
    return pltpu.CompilerParams(dimension_semantics=sem, vmem_limit_bytes=VMEM_LIMIT)


def _rms(x, g):
    ms = jnp.mean(x * x, axis=-1, keepdims=True)
    return x * lax.rsqrt(ms + NORM_EPS) * g


def _dot(a, b):
    return jnp.dot(a, b, preferred_element_type=F32)


def _dot_nt(a, b):
    return lax.dot_general(a, b, (((1,), (1,)), ((), ())), preferred_element_type=F32)


def _split(a):
    hi = a.astype(BF16)
    lo = (a - hi.astype(F32)).astype(BF16)
    return hi, lo


def _dot_split(a, b):
    ah, al = _split(a)
    bh, bl = _split(b)
    return _dot(ah, bh) + _dot(al, bh) + _dot(ah, bl)


def _top_k_mask(gate, valid, axis):
    g = jnp.where(valid, gate, NEG)
    iota = lax.broadcasted_iota(jnp.int32, g.shape, axis).astype(F32)
    sel = jnp.zeros(g.shape, dtype=jnp.bool_)
    for _ in range(MOBA_TOPK):
        mx = jnp.max(g, axis=axis, keepdims=True)
        idx = jnp.min(jnp.where(g == mx, iota, 1e9), axis=axis, keepdims=True)
        pick = iota == idx
        sel = jnp.logical_or(sel, pick)
        g = jnp.where(pick, -jnp.inf, g)
    return jnp.logical_and(sel, valid)


def _proj_body(x_ref, nm_ref, w_ref, gains_ref, cos_ref, sin_ref, gmat_ref):
    x = x_ref[...]
    h = _rms(x, nm_ref[...]).astype(BF16)
    cos = jnp.concatenate([cos_ref[...]] * 4, axis=1)
    sin = jnp.concatenate([sin_ref[...]] * 4, axis=1)
    lane = lax.broadcasted_iota(jnp.int32, (1, SEG), 1)
    first_half = (lane % HEAD) < (HEAD // 2)

    def seg(i):
        return _dot(h, w_ref[:, i * SEG:(i + 1) * SEG])

    def norm_rope(z, gain):
        ms = _dot(z * z, gmat_ref[...])
        zn = z * lax.rsqrt(ms + NORM_EPS) * gain
        swapped = jnp.where(first_half, pltpu.roll(zn, SEG - HEAD // 2, 1),
                            pltpu.roll(zn, HEAD // 2, 1))
        return zn * cos + swapped * sin

    mq = norm_rope(seg(0), gains_ref[0:1, :])
    mk = norm_rope(seg(1), gains_ref[1:2, :])
    mv = seg(2)
    dq = norm_rope(seg(3), gains_ref[2:3, :])
    dk = norm_rope(seg(4), gains_ref[3:4, :])
    dv = seg(5)
    zc = seg(6)
    cq = jnp.concatenate(
        [_rms(zc[:, hh * MEM_DIM:(hh + 1) * MEM_DIM], gains_ref[4:5, hh * MEM_DIM:(hh + 1) * MEM_DIM])
         for hh in range(MEM_HEADS)], axis=1)
    return mq, mk, mv, dq, dk, dv, cq


def _proj_kernel(x_ref, nm_ref, w_ref, gains_ref, cos_ref, sin_ref, gmat_ref,
                 mq_ref, mk_ref, mv_ref, dq_ref, dk_ref, dv_ref, cq_ref):
    outs = _proj_body(x_ref, nm_ref, w_ref, gains_ref, cos_ref, sin_ref, gmat_ref)
    for ref, val in zip((mq_ref, mk_ref, mv_ref, dq_ref, dk_ref, dv_ref, cq_ref), outs):
        ref[...] = val


def _proj_prompt_kernel(x_ref, nm_ref, w_ref, gains_ref, cos_ref, sin_ref, gmat_ref,
                        mk_ref, mkt_ref, mvt_ref, mqt_ref, dk_ref, dv_ref, dqt_ref, dvt_ref, cq_ref):
    mq, mk, mv, dq, dk, dv, cq = _proj_body(x_ref, nm_ref, w_ref, gains_ref, cos_ref, sin_ref, gmat_ref)
    tm = mq.shape[0]
    mk_ref[0] = mk
    mkt_ref[0] = mk.T.reshape(MOBA_HEADS, HEAD, tm)
    mvt_ref[0] = mv.T.reshape(MOBA_HEADS, HEAD, tm)
    mqt_ref[0] = (mq * (HEAD ** -0.5)).T
    dk_ref[0] = dk
    dv_ref[0] = dv
    dqt_ref[0] = (dq * (HEAD ** -0.5)).T
    dvt_ref[0] = dv.T
    cq_ref[0] = cq


def _proj_in_specs(tm, d, d_in, n_pos_tiles):
    full = lambda i: (0, 0)
    pos = lambda i: (i % n_pos_tiles, 0)
    return [pl.BlockSpec((tm, d), lambda i: (i, 0)),
            pl.BlockSpec((1, d), full),
            pl.BlockSpec((d, d_in), full),
            pl.BlockSpec((8, SEG), full),
            pl.BlockSpec((tm, 128), pos),
            pl.BlockSpec((tm, 128), pos),
            pl.BlockSpec((SEG, SEG), full)]


def _project(x2d, norm_mix, w_in_bf16, gains, cos_t, sin_t, gmat):
    n, d = x2d.shape
    out = jax.ShapeDtypeStruct((n, SEG), F32)
    return pl.pallas_call(
        _proj_kernel,
        grid=(1,),
        in_specs=_proj_in_specs(n, d, w_in_bf16.shape[1], 1),
        out_specs=[pl.BlockSpec((n, SEG), lambda i: (i, 0))] * 7,
        out_shape=[out] * 7,
        compiler_params=_cparams(("parallel",)),
        name="proj",
    )(x2d, norm_mix, w_in_bf16, gains, cos_t, sin_t, gmat)


def _project_prompt(x2d, b, s, norm_mix, w_in_bf16, gains, cos_t, sin_t, gmat):
    n, d = x2d.shape
    tm = TOKEN_TILE
    npt = s // tm
    std = jax.ShapeDtypeStruct((b, s, SEG), F32)
    tr = jax.ShapeDtypeStruct((b, SEG, s), F32)
    hd = jax.ShapeDtypeStruct((b, MOBA_HEADS, HEAD, s), F32)
    std_spec = pl.BlockSpec((1, tm, SEG), lambda i: (i // npt, i % npt, 0))
    tr_spec = pl.BlockSpec((1, SEG, tm), lambda i: (i // npt, 0, i % npt))
    hd_spec = pl.BlockSpec((1, MOBA_HEADS, HEAD, tm), lambda i: (i // npt, 0, 0, i % npt))
    return pl.pallas_call(
        _proj_prompt_kernel,
        grid=(n // tm,),
        in_specs=_proj_in_specs(tm, d, w_in_bf16.shape[1], npt),
        out_specs=[std_spec, hd_spec, hd_spec, tr_spec, std_spec, std_spec, tr_spec, tr_spec, std_spec],
        out_shape=[std, hd, hd, tr, std, std, tr, tr, std],
        compiler_params=_cparams(("parallel",)),
        name="proj_prompt",
    )(x2d, norm_mix, w_in_bf16, gains, cos_t, sin_t, gmat)


def _memkv_kernel(mem_ref, mn_ref, w_ref, kn_ref, k_ref, v_ref):
    m = _rms(mem_ref[0], mn_ref[...]).astype(BF16)
    z = _dot(m, w_ref[...])
    for hh in range(MEM_HEADS):
        sl = slice(hh * MEM_DIM, (hh + 1) * MEM_DIM)
        k_ref[0, :, sl] = _rms(z[:, sl], kn_ref[...])
    v_ref[0] = z[:, SEG:]


def _mem_kv(mem, mem_norm, w_mem_kv_bf16, mem_kn):
    b, m, d = mem.shape
    out = jax.ShapeDtypeStruct((b, m, SEG), F32)
    return pl.pallas_call(
        _memkv_kernel,
        grid=(b,),
        in_specs=[pl.BlockSpec((1, m, d), lambda i: (i, 0, 0)),
                  pl.BlockSpec((1, d), lambda i: (0, 0)),
                  pl.BlockSpec((d, 2 * SEG), lambda i: (0, 0)),
                  pl.BlockSpec((1, MEM_DIM), lambda i: (0, 0))],
        out_specs=[pl.BlockSpec((1, m, SEG), lambda i: (i, 0, 0))] * 2,
        out_shape=[out, out],
        compiler_params=_cparams(("parallel",)),
        name="mem_kv",
    )(mem, mem_norm, w_mem_kv_bf16, mem_kn)


def _mem_attn_kernel(q_ref, k_ref, v_ref, o_ref):
    scale = MEM_DIM ** -0.5
    for hh in range(MEM_HEADS):
        sl = slice(hh * MEM_DIM, (hh + 1) * MEM_DIM)
        s = _dot_nt(q_ref[0, :, sl], k_ref[0, :, sl]) * scale
        p = jnp.exp(s - jnp.max(s, axis=1, keepdims=True))
        o_ref[0, :, sl] = _dot(p, v_ref[0, :, sl]) / jnp.sum(p, axis=1, keepdims=True)


def _mem_attend(q, k, v, tq):
    b, t, _ = q.shape
    m = k.shape[1]
    return pl.pallas_call(
        _mem_attn_kernel,
        grid=(b, t // tq),
        in_specs=[pl.BlockSpec((1, tq, SEG), lambda i, j: (i, j, 0)),
                  pl.BlockSpec((1, m, SEG), lambda i, j: (i, 0, 0)),
                  pl.BlockSpec((1, m, SEG), lambda i, j: (i, 0, 0))],
        out_specs=pl.BlockSpec((1, tq, SEG), lambda i, j: (i, j, 0)),
        out_shape=jax.ShapeDtypeStruct((b, t, SEG), F32),
        compiler_params=_cparams(("parallel", "parallel")),
        name="mem_attn",
    )(q, k, v)


def _interleaved_head_mask(rows, cols, heads):
    per_head = rows // heads
    r = lax.broadcasted_iota(jnp.int32, (rows, cols), 0) // per_head
    c = lax.broadcasted_iota(jnp.int32, (rows, cols), 1) % heads
    return r == c


def _mem_sample_kernel(q_ref, k_ref, v_ref, o_ref):
    q8 = q_ref[0]
    qx = jnp.concatenate([q8[:, hh * MEM_DIM:(hh + 1) * MEM_DIM] for hh in range(MEM_HEADS)], axis=0)
    s = _dot_nt(qx, k_ref[...]) * (MEM_DIM ** -0.5)
    s = jnp.where(_interleaved_head_mask(s.shape[0], s.shape[1], MEM_HEADS), s, NEG)
    p = jnp.exp(s - jnp.max(s, axis=1, keepdims=True))
    o = _dot(p, v_ref[...]) / jnp.sum(p, axis=1, keepdims=True)
    for hh in range(MEM_HEADS):
        o_ref[0, :, hh * MEM_DIM:(hh + 1) * MEM_DIM] = o[hh * T_PAD:(hh + 1) * T_PAD, :]


def _mem_sample(q, cache_k, cache_v, layer):
    bd = q.shape[0]
    rows = cache_k.shape[2]
    tok = pl.BlockSpec((1, T_PAD, SEG), lambda i: (i, 0, 0))
    kv = pl.BlockSpec((None, None, rows, MEM_DIM), lambda i: (layer, i, 0, 0))
    return pl.pallas_call(
        _mem_sample_kernel,
        grid=(bd,),
        in_specs=[tok, kv, kv],
        out_specs=tok,
        out_shape=jax.ShapeDtypeStruct((bd, T_PAD, SEG), F32),
        compiler_params=_cparams(("parallel",)),
        name="mem_sample",
    )(q, cache_k, cache_v)


def _flash_step(st, kn, vtn, qt, bias, mask):
    m_old, l_old, acc = st
    s = _dot(kn, qt)
    if bias is not None:
        rows = s.shape[0] // len(bias)
        s = jnp.concatenate([s[i * rows:(i + 1) * rows] + bi for i, bi in enumerate(bias)], axis=0)
    if mask is not None:
        s = jnp.where(mask, s, NEG)
    m_new = jnp.maximum(m_old, jnp.max(s, axis=0, keepdims=True))
    alpha = jnp.exp(m_old - m_new)
    p = jnp.exp(s - m_new)
    l_new = alpha * l_old + jnp.sum(p, axis=0, keepdims=True)
    return m_new, l_new, alpha * acc + _dot(vtn, p)


def _moba_prompt_kernel(qt_ref, k_ref, vt_ref, o_ref, kmean_ref, bias_ref, *, nb, grp):
    j = pl.program_id(2)
    blk = MOBA_BLOCK
    pair = 2 * HEAD

    @pl.when(j == 0)
    def _():
        for n in range(nb):
            kmean_ref[n:n + 1, :] = jnp.mean(k_ref[0, n * blk:(n + 1) * blk, :], axis=0, keepdims=True)

    qt = qt_ref[0]
    row_head = lax.broadcasted_iota(jnp.int32, (pair, 1), 0) // HEAD
    lane_head = lax.broadcasted_iota(jnp.int32, (1, pair), 1) // HEAD
    qts = [jnp.where(row_head == hh, qt, 0.0) for hh in range(2)]
    valid = lax.broadcasted_iota(jnp.int32, (nb, blk), 0) < j
    for hh in range(2):
        gate = _dot_split(jnp.where(lane_head == hh, kmean_ref[...], 0.0), qt)
        bias_ref[hh] = jnp.where(_top_k_mask(gate, valid, 0), 0.0, NEG)

    def init():
        return (jnp.full((1, blk), NEG, F32), jnp.zeros((1, blk), F32), jnp.zeros((HEAD, blk), F32))

    key = lax.broadcasted_iota(jnp.int32, (blk, blk), 0)
    qry = lax.broadcasted_iota(jnp.int32, (blk, blk), 1)
    own = pl.ds(pl.multiple_of(j * blk, blk), blk)
    k_own = k_ref[0, own, :]
    state = tuple(_flash_step(init(), k_own, vt_ref[0, hh, :, own], qts[hh], None, key <= qry)
                  for hh in range(2))

    def body(g, st):
        rows = pl.ds(pl.multiple_of(g * grp * blk, grp * blk), grp * blk)
        kn = k_ref[0, rows, :]
        return tuple(_flash_step(st[hh], kn, vt_ref[0, hh, :, rows], qts[hh],
                                 [bias_ref[hh, pl.ds(g * grp + i, 1), :] for i in range(grp)], None)
                     for hh in range(2))

    state = lax.fori_loop(0, (j + grp - 1) // grp, body, state)
    ot = jnp.concatenate([acc / l for (_, l, acc) in state], axis=0)
    o_ref[0] = ot.T


def _moba_prompt(qt, k, vt):
    b, s, _ = k.shape
    nb = s // MOBA_BLOCK
    pair = 2 * HEAD
    grp = math.gcd(nb, KV_GROUP)
    return pl.pallas_call(
        functools.partial(_moba_prompt_kernel, nb=nb, grp=grp),
        grid=(b, SEG // pair, nb),
        in_specs=[pl.BlockSpec((1, pair, MOBA_BLOCK), lambda i, g, j: (i, g, j)),
                  pl.BlockSpec((1, s, pair), lambda i, g, j: (i, 0, g)),
                  pl.BlockSpec((1, 2, HEAD, s), lambda i, g, j: (i, g, 0, 0))],
        out_specs=pl.BlockSpec((1, MOBA_BLOCK, pair), lambda i, g, j: (i, j, g)),
        out_shape=jax.ShapeDtypeStruct((b, s, SEG), F32),
        scratch_shapes=[pltpu.VMEM((nb, pair), F32),
                        pltpu.VMEM((2, nb, MOBA_BLOCK), F32)],
        compiler_params=_cparams(("parallel", "parallel", "arbitrary")),
        name="moba_prompt",
    )(qt, k, vt)


def _diff_lambda(lam_ref, lam_init):
    lam = lam_ref[...]
    a = jnp.sum(lam[0:1, :] * lam[1:2, :], axis=1, keepdims=True)
    b = jnp.sum(lam[2:3, :] * lam[3:4, :], axis=1, keepdims=True)
    return jnp.exp(a) - jnp.exp(b) + lam_init


def _diff_prompt_kernel(qt_ref, k_ref, vt_ref, lam_ref, g_ref, o_ref, *, lam_init, tk):
    j = pl.program_id(2)
    tq = QUERY_TILE
    hw = 2 * HEAD
    row_map = lax.broadcasted_iota(jnp.int32, (hw, 1), 0) // HEAD
    qt2 = []
    for hh in range(2):
        qt = qt_ref[0, hh * hw:(hh + 1) * hw, :]
        qt2.append(jnp.concatenate([jnp.where(row_map == 0, qt, 0.0), jnp.where(row_map == 1, qt, 0.0)], axis=1))

    def tile(n, st, mask):
        rows = pl.ds(pl.multiple_of(n * tk, tk), tk)
        return tuple(_flash_step(st[hh], k_ref[0, rows, hh * hw:(hh + 1) * hw],
                                 vt_ref[0, hh * DIFF_VDIM:(hh + 1) * DIFF_VDIM, rows], qt2[hh], None, mask)
                     for hh in range(2))

    init = (jnp.full((1, 2 * tq), NEG, F32), jnp.zeros((1, 2 * tq), F32), jnp.zeros((DIFF_VDIM, 2 * tq), F32))
    last = (j * tq) // tk
    state = lax.fori_loop(0, last, lambda n, st: tile(n, st, None), (init, init))
    key = last * tk + lax.broadcasted_iota(jnp.int32, (tk, 2 * tq), 0)
    qry = j * tq + lax.broadcasted_iota(jnp.int32, (tk, 2 * tq), 1) % tq
    state = tile(last, state, key <= qry)

    lam = _diff_lambda(lam_ref, lam_init)
    for hh in range(2):
        _, l, acc = state[hh]
        ot = acc / l
        d = (ot[:, :tq] - lam * ot[:, tq:]).T
        o_ref[0, :, hh * DIFF_VDIM:(hh + 1) * DIFF_VDIM] = _rms(d, g_ref[...]) * (1.0 - lam_init)


def _diff_prompt(qt, k, vt, lam, subln, lam_init):
    b, s, _ = k.shape
    pw = 2 * DIFF_VDIM
    tq = QUERY_TILE
    tk = DIFF_KEY_TILE
    return pl.pallas_call(
        functools.partial(_diff_prompt_kernel, lam_init=lam_init, tk=tk),
        grid=(b, DIFF_HEADS // 2, s // tq),
        in_specs=[pl.BlockSpec((1, pw, tq), lambda i, h, j: (i, h, j)),
                  pl.BlockSpec((1, s, pw), lambda i, h, j: (i, 0, h)),
                  pl.BlockSpec((1, pw, s), lambda i, h, j: (i, h, 0)),
                  pl.BlockSpec((4, HEAD), lambda i, h, j: (0, 0)),
                  pl.BlockSpec((1, DIFF_VDIM), lambda i, h, j: (0, 0))],
        out_specs=pl.BlockSpec((1, tq, pw), lambda i, h, j: (i, j, h)),
        out_shape=jax.ShapeDtypeStruct((b, s, SEG), F32),
        compiler_params=_cparams(("parallel", "parallel", "arbitrary")),
        name="diff_prompt",
    )(qt, k, vt, lam, subln)


def _merge_kernel(x_ref, om_ref, od_ref, oc_ref, nm_ref, wg_ref, bg_ref, wb_ref, wo_ref, y_ref):
    x = x_ref[...]
    d = x.shape[1]
    h = _rms(x, nm_ref[...]).astype(BF16)
    merged = jnp.zeros_like(x)
    for n, o_ref in enumerate((om_ref, od_ref, oc_ref)):
        sl = slice(n * d, (n + 1) * d)
        g = jax.nn.sigmoid(_dot(h, wg_ref[:, sl]) + bg_ref[:, sl])
        merged = merged + g * _dot(o_ref[...].astype(BF16), wb_ref[n])
    y_ref[...] = x + _dot(merged.astype(BF16), wo_ref[...])


def _merge(x2d, o_moba, o_diff, o_mem, norm_mix, w_gate, b_gate, w_branch, w_out, tm):
    n, d = x2d.shape
    tok = lambda i: (i, 0)
    full = lambda i: (0, 0)
    return pl.pallas_call(
        _merge_kernel,
        grid=(n // tm,),
        in_specs=[pl.BlockSpec((tm, d), tok),
                  pl.BlockSpec((tm, SEG), tok),
                  pl.BlockSpec((tm, SEG), tok),
                  pl.BlockSpec((tm, SEG), tok),
                  pl.BlockSpec((1, d), full),
                  pl.BlockSpec((d, 3 * d), full),
                  pl.BlockSpec((1, 3 * d), full),
                  pl.BlockSpec((3, SEG, d), lambda i: (0, 0, 0)),
                  pl.BlockSpec((d, d), full)],
        out_specs=pl.BlockSpec((tm, d), tok),
        out_shape=jax.ShapeDtypeStruct((n, d), F32),
        compiler_params=_cparams(("parallel",)),
        name="merge",
    )(x2d, o_moba, o_diff, o_mem, norm_mix, w_gate, b_gate, w_branch, w_out)


def _ffn_kernel(x_ref, nf_ref, wu_ref, wd_ref, y_ref, *, n_chunks):
    x = x_ref[...]
    hn = _rms(x, nf_ref[...]).astype(BF16)
    cw = wu_ref.shape[1] // n_chunks
    acc = x
    for c in range(n_chunks):
        u = jnp.square(jnp.maximum(_dot(hn, wu_ref[:, c * cw:(c + 1) * cw]), 0.0))
        acc = acc + _dot(u.astype(BF16), wd_ref[c * cw:(c + 1) * cw, :])
    y_ref[...] = acc


def _ffn(x2d, norm_ffn, w_up, w_down, tm):
    n, d = x2d.shape
    dff = w_up.shape[1]
    tok = lambda i: (i, 0)
    full = lambda i: (0, 0)
    return pl.pallas_call(
        functools.partial(_ffn_kernel, n_chunks=4),
        grid=(n // tm,),
        in_specs=[pl.BlockSpec((tm, d), tok),
                  pl.BlockSpec((1, d), full),
                  pl.BlockSpec((d, dff), full),
                  pl.BlockSpec((dff, d), full)],
        out_specs=pl.BlockSpec((tm, d), tok),
        out_shape=jax.ShapeDtypeStruct((n, d), F32),
        compiler_params=_cparams(("parallel",)),
        name="ffn",
    )(x2d, norm_ffn, w_up, w_down)


def _page_specs(layer, npg, block):
    def spec(jj):
        return pl.BlockSpec((None, None) + block,
                            lambda b, c, pt: (layer, pt[b, c * npg + jj]) + (0,) * len(block))
    return [spec(jj) for jj in range(npg)]


def _new_token_scores(qrows, knx, t_dec):
    tok = lax.broadcasted_iota(jnp.int32, (qrows.shape[0], 1), 0) % T_PAD
    return [jnp.where(tok >= t, jnp.sum(qrows * knx[t], axis=1, keepdims=True), NEG) for t in range(t_dec)]


def _diff_sample_kernel(pt_ref, q_ref, kn_ref, vn_ref, lam_ref, g_ref, *rest, npg, lam_init, t_dec):
    del pt_ref
    k_refs, v_refs = rest[:npg], rest[npg:2 * npg]
    o_ref, m_ref, l_ref, acc_ref = rest[2 * npg:]
    c = pl.program_id(1)
    nrow = DIFF_HEADS * 2 * T_PAD
    hw = 2 * HEAD

    def head_rows(x8, masked):
        lane_map = lax.broadcasted_iota(jnp.int32, (1, hw), 1) // HEAD
        out = []
        for hh in range(DIFF_HEADS):
            xs = x8[:, hh * hw:(hh + 1) * hw]
            for mp in range(2):
                out.append(jnp.where(lane_map == mp, xs, 0.0) if masked else xs)
        return jnp.concatenate(out, axis=0)

    qx = head_rows(q_ref[0] * (HEAD ** -0.5), True)

    @pl.when(c == 0)
    def _():
        m_ref[...] = jnp.full(m_ref.shape, NEG, F32)
        l_ref[...] = jnp.zeros(l_ref.shape, F32)
        acc_ref[...] = jnp.zeros(acc_ref.shape, F32)

    s = jnp.concatenate([_dot_nt(qx, k_refs[p][...]) for p in range(npg)], axis=1)
    row_head = lax.broadcasted_iota(jnp.int32, s.shape, 0) // (2 * T_PAD)
    col_head = lax.broadcasted_iota(jnp.int32, s.shape, 1) % DIFF_HEADS
    s = jnp.where(row_head == col_head, s, NEG)
    m_old = m_ref[...]
    m_new = jnp.maximum(m_old, jnp.max(s, axis=1, keepdims=True))
    alpha = jnp.exp(m_old - m_new)
    p_all = jnp.exp(s - m_new)
    pw = k_refs[0].shape[0]
    pv = _dot(p_all[:, :pw], v_refs[0][...])
    for p in range(1, npg):
        pv = pv + _dot(p_all[:, p * pw:(p + 1) * pw], v_refs[p][...])
    l_ref[...] = alpha * l_ref[...] + jnp.sum(p_all, axis=1, keepdims=True)
    acc_ref[...] = alpha * acc_ref[...] + pv
    m_ref[...] = m_new

    @pl.when(c == pl.num_programs(1) - 1)
    def _():
        kn, vn = kn_ref[0], vn_ref[0]
        knx = [head_rows(jnp.broadcast_to(kn[t:t + 1, :], (T_PAD, SEG)), False) for t in range(t_dec)]
        vnx = [head_rows(jnp.broadcast_to(vn[t:t + 1, :], (T_PAD, SEG)), False) for t in range(t_dec)]
        sn = _new_token_scores(qx, knx, t_dec)
        m0 = m_ref[...]
        m1 = m0
        for t in range(t_dec):
            m1 = jnp.maximum(m1, sn[t])
        a = jnp.exp(m0 - m1)
        l = a * l_ref[...]
        acc = a * acc_ref[...]
        for t in range(t_dec):
            pt = jnp.exp(sn[t] - m1)
            l = l + pt
            acc = acc + pt * vnx[t]
        o = acc / l
        lam = _diff_lambda(lam_ref, lam_init)
        for hh in range(DIFF_HEADS):
            r0 = 2 * hh * T_PAD
            d = o[r0:r0 + T_PAD, :] - lam * o[r0 + T_PAD:r0 + 2 * T_PAD, :]
            o_ref[0, :, hh * DIFF_VDIM:(hh + 1) * DIFF_VDIM] = _rms(d, g_ref[...]) * (1.0 - lam_init)


def _diff_sample(page_table, q, kn, vn, lam, subln, cache_k, cache_v, layer, lam_init, t_dec):
    bd = q.shape[0]
    n_pages = page_table.shape[1]
    rows = cache_k.shape[2]
    npg = min(PAGES_PER_STEP, n_pages)
    nrow = DIFF_HEADS * 2 * T_PAD
    tokspec = pl.BlockSpec((1, T_PAD, SEG), lambda b, c, pt: (b, 0, 0))
    grid_spec = pltpu.PrefetchScalarGridSpec(
        num_scalar_prefetch=1,
        grid=(bd, n_pages // npg),
        in_specs=[tokspec, tokspec, tokspec,
                  pl.BlockSpec((4, HEAD), lambda b, c, pt: (0, 0)),
                  pl.BlockSpec((1, DIFF_VDIM), lambda b, c, pt: (0, 0))]
                 + _page_specs(layer, npg, (rows, 128)) + _page_specs(layer, npg, (rows, 128)),
        out_specs=tokspec,
        scratch_shapes=[pltpu.VMEM((nrow, 1), F32),
                        pltpu.VMEM((nrow, 1), F32),
                        pltpu.VMEM((nrow, DIFF_VDIM), F32)])
    return pl.pallas_call(
        functools.partial(_diff_sample_kernel, npg=npg, lam_init=lam_init, t_dec=t_dec),
        grid_spec=grid_spec,
        out_shape=jax.ShapeDtypeStruct((bd, T_PAD, SEG), F32),
        compiler_params=_cparams(("parallel", "arbitrary")),
        name="diff_sample",
    )(page_table, q, kn, vn, lam, subln, *([cache_k] * npg), *([cache_v] * npg))


def _group_query(q8):
    rows = lax.broadcasted_iota(jnp.int32, (MOBA_HEADS * T_PAD, SEG), 0) // T_PAD
    lanes = lax.broadcasted_iota(jnp.int32, (MOBA_HEADS * T_PAD, SEG), 1) // HEAD
    return jnp.where(rows == lanes, jnp.concatenate([q8] * MOBA_HEADS, axis=0), 0.0)


def _moba_sample_kernel(pt_ref, q_ref, kn_ref, vn_ref, *rest, npg, t_dec, nblk):
    del pt_ref
    k_refs, v_refs = rest[:npg], rest[npg:2 * npg]
    o_ref, ks_ref, pm_ref, pl_ref, po_ref = rest[2 * npg:]
    c = pl.program_id(1)
    qg = _group_query(q_ref[0] * (HEAD ** -0.5))
    pg = k_refs[0].shape[2]
    ppb = MOBA_BLOCK // pg
    bpc = npg // ppb
    lane = lax.broadcasted_iota(jnp.int32, (1, 128), 1)

    @pl.when(c == 0)
    def _():
        ks_ref[...] = jnp.zeros(ks_ref.shape, F32)
        pm_ref[...] = jnp.zeros(pm_ref.shape, F32)
        pl_ref[...] = jnp.zeros(pl_ref.shape, F32)

    for bb in range(bpc):
        kts = [k_refs[bb * ppb + i][...].reshape(SEG, pg) for i in range(ppb)]
        vts = [v_refs[bb * ppb + i][...].reshape(SEG, pg) for i in range(ppb)]
        s = jnp.concatenate([_dot(qg, kt) for kt in kts], axis=1)
        m = jnp.max(s, axis=1, keepdims=True)
        p = jnp.exp(s - m)
        l = jnp.sum(p, axis=1, keepdims=True)
        o = _dot_nt(p[:, :pg], vts[0])
        ksum = kts[0]
        for i in range(1, ppb):
            o = o + _dot_nt(p[:, i * pg:(i + 1) * pg], vts[i])
            ksum = ksum + kts[i]
        ksum = jnp.sum(ksum, axis=1, keepdims=True)
        n = c * bpc + bb
        ks_ref[...] = jnp.where(lane == n, ksum, ks_ref[...])
        pm_ref[...] = jnp.where(lane == n, m, pm_ref[...])
        pl_ref[...] = jnp.where(lane == n, l, pl_ref[...])
        po_ref[n] = o

    @pl.when(c == pl.num_programs(1) - 1)
    def _():
        gate = _dot_split(qg, ks_ref[...])
        sel = _top_k_mask(gate, lane < nblk, 1)
        kn, vn = kn_ref[0], vn_ref[0]
        sn = _new_token_scores(qg, [kn[t:t + 1, :] for t in range(t_dec)], t_dec)
        pm = jnp.where(sel, pm_ref[...], NEG)
        mt = jnp.max(pm, axis=1, keepdims=True)
        for t in range(t_dec):
            mt = jnp.maximum(mt, sn[t])
        w = jnp.where(sel, jnp.exp(pm - mt), 0.0)
        l = jnp.sum(w * pl_ref[...], axis=1, keepdims=True)
        o = jnp.zeros((MOBA_HEADS * T_PAD, SEG), F32)
        for n in range(nblk):
            o = o + w[:, n:n + 1] * po_ref[n]
        for t in range(t_dec):
            pt = jnp.exp(sn[t] - mt)
            l = l + pt
            o = o + pt * vn[t:t + 1, :]
        o = o / l
        lane_head = lax.broadcasted_iota(jnp.int32, (T_PAD, SEG), 1) // HEAD
        out = jnp.zeros((T_PAD, SEG), F32)
        for hh in range(MOBA_HEADS):
            out = jnp.where(lane_head == hh, o[hh * T_PAD:(hh + 1) * T_PAD, :], out)
        o_ref[0] = out


def _moba_sample(page_table, q, kn, vn, cache_kt, cache_vt, layer, t_dec):
    bd = q.shape[0]
    n_pages = page_table.shape[1]
    page = cache_kt.shape[4]
    npg = min(PAGES_PER_STEP, n_pages)
    nblk = n_pages * page // MOBA_BLOCK
    nrow = MOBA_HEADS * T_PAD
    tokspec = pl.BlockSpec((1, T_PAD, SEG), lambda b, c, pt: (b, 0, 0))
    blk = (MOBA_HEADS, HEAD, page)
    grid_spec = pltpu.PrefetchScalarGridSpec(
        num_scalar_prefetch=1,
        grid=(bd, n_pages // npg),
        in_specs=[tokspec, tokspec, tokspec] + _page_specs(layer, npg, blk) + _page_specs(layer, npg, blk),
        out_specs=tokspec,
        scratch_shapes=[pltpu.VMEM((SEG, 128), F32),
                        pltpu.VMEM((nrow, 128), F32),
                        pltpu.VMEM((nrow, 128), F32),
                        pltpu.VMEM((nblk, nrow, SEG), F32)])
    return pl.pallas_call(
        functools.partial(_moba_sample_kernel, npg=npg, t_dec=t_dec, nblk=nblk),
        grid_spec=grid_spec,
        out_shape=jax.ShapeDtypeStruct((bd, T_PAD, SEG), F32),
        compiler_params=_cparams(("parallel", "arbitrary")),
        name="moba_sample",
    )(page_table, q, kn, vn, *([cache_kt] * npg), *([cache_vt] * npg))


def _rope_tables(pos):
    half = HEAD // 2
    inv = ROPE_THETA ** (-2.0 * jnp.arange(half, dtype=F32) / HEAD)
    ang = pos.astype(F32)[:, None] * inv[None, :]
    cos, sin = jnp.cos(ang), jnp.sin(ang)
    return jnp.concatenate([cos] * 4, axis=1), jnp.concatenate([-sin, sin] * 2, axis=1)


def _block_diag(width, group, value, dtype):
    i = jnp.arange(width) // group
    return jnp.where(i[:, None] == i[None, :], value, 0.0).astype(dtype)


def kernel(x_prompt, x_sample, cache_moba_k, cache_moba_v, cache_diff_k, cache_diff_v, cache_mem_k, cache_mem_v, page_table, mem_prompt, norm_mix, w_in, moba_qn, moba_kn, diff_qn, diff_kn, diff_lambda, diff_subln, mem_norm, w_mem_kv, mem_qn, mem_kn, w_branch, w_gate, b_gate, w_out, norm_ffn, w_up, w_down):
    depth = w_in.shape[0]
    b, s, d = x_prompt.shape
    bd, t_dec, _ = x_sample.shape
    n_pool, page = cache_moba_k.shape[1], cache_moba_k.shape[2]
    n_pages = page_table.shape[1]
    past = n_pages * page
    assert d == 2 * SEG and w_in.shape[2] == 7 * SEG
    assert s % TOKEN_TILE == 0 and past % MOBA_BLOCK == 0 and MOBA_BLOCK % page == 0 and page % 128 == 0
    assert t_dec <= T_PAD and n_pages % min(PAGES_PER_STEP, n_pages) == 0
    assert MOBA_TOPK <= past // MOBA_BLOCK <= 128

    tm = TOKEN_TILE
    cos_p, sin_p = _rope_tables(jnp.arange(s, dtype=jnp.int32))
    cos_s, sin_s = _rope_tables(past + jnp.arange(T_PAD, dtype=jnp.int32))
    cos_s, sin_s = jnp.tile(cos_s, (bd, 1)), jnp.tile(sin_s, (bd, 1))
    gmat = _block_diag(SEG, HEAD, 1.0 / HEAD, F32)

    w_in_b, w_mem_b = w_in.astype(BF16), w_mem_kv.astype(BF16)
    w_gate_b, w_branch_b, w_out_b = w_gate.astype(BF16), w_branch.astype(BF16), w_out.astype(BF16)
    w_up_b, w_down_b = w_up.astype(BF16), w_down.astype(BF16)

    c_mkt = jnp.transpose(cache_moba_k, (0, 1, 3, 4, 2))
    c_mvt = jnp.transpose(cache_moba_v, (0, 1, 3, 4, 2))
    c_dk = cache_diff_k.reshape(depth, n_pool, page * DIFF_HEADS, 2 * HEAD)
    c_dv = cache_diff_v.reshape(depth, n_pool, page * DIFF_HEADS, DIFF_VDIM)
    c_ck = cache_mem_k.reshape(depth, bd, -1, MEM_DIM)
    c_cv = cache_mem_v.reshape(depth, bd, -1, MEM_DIM)

    xp = x_prompt.reshape(b * s, d)
    xs = jnp.pad(x_sample, ((0, 0), (0, T_PAD - t_dec), (0, 0))).reshape(bd * T_PAD, d)
    ns = bd * T_PAD
    outs = [[] for _ in range(10)]
    for l in range(depth):
        lam_init = 0.8 - 0.6 * math.exp(-0.3 * l)
        gains = jnp.concatenate([jnp.tile(moba_qn[l], 8)[None], jnp.tile(moba_kn[l], 8)[None],
                                 jnp.tile(diff_qn[l], 8)[None], jnp.tile(diff_kn[l], 8)[None],
                                 jnp.tile(mem_qn[l], 4)[None], jnp.zeros((3, SEG), F32)], axis=0)
        nm, nf = norm_mix[l][None], norm_ffn[l][None]
        subln = diff_subln[l][None]

        mk, mkt, mvt, mqt, dk, dv, dqt, dvt, cq = _project_prompt(
            xp, b, s, nm, w_in_b[l], gains, cos_p, sin_p, gmat)
        o_moba = _moba_prompt(mqt, mk, mvt)
        o_diff = _diff_prompt(dqt, dk, dvt, diff_lambda[l], subln, lam_init)
        m_k, m_v = _mem_kv(mem_prompt, mem_norm[l][None], w_mem_b[l], mem_kn[l][None])
        o_mem = _mem_attend(cq, m_k, m_v, 1024 if s % 1024 == 0 else TOKEN_TILE)
        xp = _merge(xp, o_moba.reshape(b * s, SEG), o_diff.reshape(b * s, SEG), o_mem.reshape(b * s, SEG),
                    nm, w_gate_b[l], b_gate[l][None], w_branch_b[l], w_out_b[l], tm)
        xp = _ffn(xp, nf, w_up_b[l], w_down_b[l], tm)
        for lst, a in zip(outs[:6], (mkt, mvt, dk, dv, m_k, m_v)):
            lst.append(a)

        smq, smk, smv, sdq, sdk, sdv, scq = _project(xs, nm, w_in_b[l], gains, cos_s, sin_s, gmat)
        t3 = lambda a: a.reshape(bd, T_PAD, SEG)
        so_moba = _moba_sample(page_table, t3(smq), t3(smk), t3(smv), c_mkt, c_mvt, l, t_dec)
        so_diff = _diff_sample(page_table, t3(sdq), t3(sdk), t3(sdv), diff_lambda[l], subln,
                               c_dk, c_dv, l, lam_init, t_dec)
        so_mem = _mem_sample(t3(scq), c_ck, c_cv, l)
        xs = _merge(xs, so_moba.reshape(ns, SEG), so_diff.reshape(ns, SEG), so_mem.reshape(ns, SEG),
                    nm, w_gate_b[l], b_gate[l][None], w_branch_b[l], w_out_b[l], ns)
        xs = _ffn(xs, nf, w_up_b[l], w_down_b[l], ns)
        for lst, a in zip(outs[6:], (smk, smv, sdk, sdv)):
            lst.append(t3(a)[:, :t_dec])

    st = lambda lst, shape: jnp.stack(lst).reshape((depth,) + shape)
    tr = lambda lst: jnp.transpose(jnp.stack(lst), (0, 1, 4, 2, 3))
    return (xp.reshape(b, s, d),
            xs.reshape(bd, T_PAD, d)[:, :t_dec],
            tr(outs[0]), tr(outs[1]),
            st(outs[2], (b, s, DIFF_HEADS, 2 * HEAD)), st(outs[3], (b, s, DIFF_HEADS, DIFF_VDIM)),
            st(outs[4], (b, -1, MEM_HEADS, MEM_DIM)), st(outs[5], (b, -1, MEM_HEADS, MEM_DIM)),
            st(outs[6], (bd, t_dec, 8, HEAD)), st(outs[7], (bd, t_dec, 8, HEAD)),
            st(outs[8], (bd, t_dec, DIFF_HEADS, 2 * HEAD)), st(outs[9], (bd, t_dec, DIFF_HEADS, DIFF_VDIM)))
```

```python
import functools
import math

import jax
import jax.numpy as jnp
from jax import lax
from jax.experimental import pallas as pl
from jax.experimental.pallas import tpu as pltpu

F32 = jnp.float32
BF16 = jnp.bfloat16

NEG = -1e30
NORM_EPS = 1e-6
ROPE_THETA = 10000.0
SEG = 512
HEAD = 64
Q_SCALE = HEAD ** -0.5 * math.log2(math.e)
MOBA_HEADS = 8
MOBA_BLOCK = 256
MOBA_TOPK = 3
MEM_HEADS = 4
MEM_DIM = 128
DIFF_HEADS = 4
DIFF_VDIM = 128
T_PAD = 8
PAGES_PER_STEP = 16
TOKEN_TILE = 512
QUERY_TILE = 256
KV_GROUP = 4
DIFF_KEY_TILE = 512
VMEM_LIMIT = 56 * 1024 * 1024


def _cparams(sem):
    return pltpu.CompilerParams(dimension_semantics=sem, vmem_limit_bytes=VMEM_LIMIT)


def _rms(x, g):
    ms = jnp.mean(x * x, axis=-1, keepdims=True)
    return x * lax.rsqrt(ms + NORM_EPS) * g


def _dot(a, b):
    return jnp.dot(a, b, preferred_element_type=F32)


def _dot_nt(a, b):
    return lax.dot_general(a, b, (((1,), (1,)), ((), ())), preferred_element_type=F32)


def _split(a):
    hi = a.astype(BF16)
    lo = (a - hi.astype(F32)).astype(BF16)
    return hi, lo


def _dot_split(a, b):
    ah, al = _split(a)
    bh, bl = _split(b)
    return _dot(ah, bh) + _dot(al, bh) + _dot(ah, bl)


def _top_k_mask(gate, valid, axis):
    g = jnp.where(valid, gate, NEG)
    iota = lax.broadcasted_iota(jnp.int32, g.shape, axis).astype(F32)
    sel = jnp.zeros(g.shape, dtype=jnp.bool_)
    for _ in range(MOBA_TOPK):
        mx = jnp.max(g, axis=axis, keepdims=True)
        idx = jnp.min(jnp.where(g == mx, iota, 1e9), axis=axis, keepdims=True)
        pick = iota == idx
        sel = jnp.logical_or(sel, pick)
        g = jnp.where(pick, -jnp.inf, g)
    return jnp.logical_and(sel, valid)


def _proj_body(x_ref, nm_ref, w_ref, gains_ref, cos_ref, sin_ref, gmat_ref):
    x = x_ref[...]
    h = _rms(x, nm_ref[...]).astype(BF16)
    cos = jnp.concatenate([cos_ref[...]] * 4, axis=1)
    sin = jnp.concatenate([sin_ref[...]] * 4, axis=1)
    lane = lax.broadcasted_iota(jnp.int32, (1, SEG), 1)
    first_half = (lane % HEAD) < (HEAD // 2)

    def seg(i):
        return _dot(h, w_ref[:, i * SEG:(i + 1) * SEG])

    def norm_rope(z, gain):
        ms = _dot(z * z, gmat_ref[...])
        zn = z * lax.rsqrt(ms + NORM_EPS) * gain
        swapped = jnp.where(first_half, pltpu.roll(zn, SEG - HEAD // 2, 1),
                            pltpu.roll(zn, HEAD // 2, 1))
        return zn * cos + swapped * sin

    mq = norm_rope(seg(0), gains_ref[0:1, :])
    mk = norm_rope(seg(1), gains_ref[1:2, :])
    mv = seg(2)
    dq = norm_rope(seg(3), gains_ref[2:3, :])
    dk = norm_rope(seg(4), gains_ref[3:4, :])
    dv = seg(5)
    zc = seg(6)
    cq = jnp.concatenate(
        [_rms(zc[:, hh * MEM_DIM:(hh + 1) * MEM_DIM], gains_ref[4:5, hh * MEM_DIM:(hh + 1) * MEM_DIM])
         for hh in range(MEM_HEADS)], axis=1)
    return mq, mk, mv, dq, dk, dv, cq


def _proj_kernel(x_ref, nm_ref, w_ref, gains_ref, cos_ref, sin_ref, gmat_ref,
                 mq_ref, mk_ref, mv_ref, dq_ref, dk_ref, dv_ref, cq_ref):
    outs = _proj_body(x_ref, nm_ref, w_ref, gains_ref, cos_ref, sin_ref, gmat_ref)
    for ref, val in zip((mq_ref, mk_ref, mv_ref, dq_ref, dk_ref, dv_ref, cq_ref), outs):
        ref[...] = val


def _proj_prompt_kernel(x_ref, nm_ref, w_ref, gains_ref, cos_ref, sin_ref, gmat_ref,
                        mk_ref, mkt_ref, mvt_ref, mqt_ref, dk_ref, dv_ref, dqt_ref, dvt_ref, cq_ref):
    mq, mk, mv, dq, dk, dv, cq = _proj_body(x_ref, nm_ref, w_ref, gains_ref, cos_ref, sin_ref, gmat_ref)
    tm = mq.shape[0]
    mk_ref[0] = mk
    mkt_ref[0] = mk.T.reshape(MOBA_HEADS, HEAD, tm)
    mvt_ref[0] = mv.T.reshape(MOBA_HEADS, HEAD, tm)
    mqt_ref[0] = (mq * Q_SCALE).T
    dk_ref[0] = dk
    dv_ref[0] = dv
    dqt_ref[0] = (dq * Q_SCALE).T
    dvt_ref[0] = dv.T
    cq_ref[0] = cq


def _proj_in_specs(tm, d, d_in, n_pos_tiles):
    full = lambda i: (0, 0)
    pos = lambda i: (i % n_pos_tiles, 0)
    return [pl.BlockSpec((tm, d), lambda i: (i, 0)),
            pl.BlockSpec((1, d), full),
            pl.BlockSpec((d, d_in), full),
            pl.BlockSpec((8, SEG), full),
            pl.BlockSpec((tm, 128), pos),
            pl.BlockSpec((tm, 128), pos),
            pl.BlockSpec((SEG, SEG), full)]


def _project(x2d, norm_mix, w_in_bf16, gains, cos_t, sin_t, gmat):
    n, d = x2d.shape
    out = jax.ShapeDtypeStruct((n, SEG), F32)
    return pl.pallas_call(
        _proj_kernel,
        grid=(1,),
        in_specs=_proj_in_specs(n, d, w_in_bf16.shape[1], 1),
        out_specs=[pl.BlockSpec((n, SEG), lambda i: (i, 0))] * 7,
        out_shape=[out] * 7,
        compiler_params=_cparams(("parallel",)),
        name="proj",
    )(x2d, norm_mix, w_in_bf16, gains, cos_t, sin_t, gmat)


def _project_prompt(x2d, b, s, norm_mix, w_in_bf16, gains, cos_t, sin_t, gmat):
    n, d = x2d.shape
    tm = TOKEN_TILE
    npt = s // tm
    std = jax.ShapeDtypeStruct((b, s, SEG), F32)
    tr = jax.ShapeDtypeStruct((b, SEG, s), F32)
    hd = jax.ShapeDtypeStruct((b, MOBA_HEADS, HEAD, s), F32)
    std_spec = pl.BlockSpec((1, tm, SEG), lambda i: (i // npt, i % npt, 0))
    tr_spec = pl.BlockSpec((1, SEG, tm), lambda i: (i // npt, 0, i % npt))
    hd_spec = pl.BlockSpec((1, MOBA_HEADS, HEAD, tm), lambda i: (i // npt, 0, 0, i % npt))
    return pl.pallas_call(
        _proj_prompt_kernel,
        grid=(n // tm,),
        in_specs=_proj_in_specs(tm, d, w_in_bf16.shape[1], npt),
        out_specs=[std_spec, hd_spec, hd_spec, tr_spec, std_spec, std_spec, tr_spec, tr_spec, std_spec],
        out_shape=[std, hd, hd, tr, std, std, tr, tr, std],
        compiler_params=_cparams(("parallel",)),
        name="proj_prompt",
    )(x2d, norm_mix, w_in_bf16, gains, cos_t, sin_t, gmat)


def _memkv_kernel(mem_ref, mn_ref, w_ref, kn_ref, k_ref, v_ref):
    m = _rms(mem_ref[0], mn_ref[...]).astype(BF16)
    z = _dot(m, w_ref[...])
    for hh in range(MEM_HEADS):
        sl = slice(hh * MEM_DIM, (hh + 1) * MEM_DIM)
        k_ref[0, :, sl] = _rms(z[:, sl], kn_ref[...])
    v_ref[0] = z[:, SEG:]


def _mem_kv(mem, mem_norm, w_mem_kv_bf16, mem_kn):
    b, m, d = mem.shape
    out = jax.ShapeDtypeStruct((b, m, SEG), F32)
    return pl.pallas_call(
        _memkv_kernel,
        grid=(b,),
        in_specs=[pl.BlockSpec((1, m, d), lambda i: (i, 0, 0)),
                  pl.BlockSpec((1, d), lambda i: (0, 0)),
                  pl.BlockSpec((d, 2 * SEG), lambda i: (0, 0)),
                  pl.BlockSpec((1, MEM_DIM), lambda i: (0, 0))],
        out_specs=[pl.BlockSpec((1, m, SEG), lambda i: (i, 0, 0))] * 2,
        out_shape=[out, out],
        compiler_params=_cparams(("parallel",)),
        name="mem_kv",
    )(mem, mem_norm, w_mem_kv_bf16, mem_kn)


def _mem_attn_kernel(q_ref, k_ref, v_ref, o_ref):
    scale = MEM_DIM ** -0.5
    for hh in range(MEM_HEADS):
        sl = slice(hh * MEM_DIM, (hh + 1) * MEM_DIM)
        s = _dot_nt(q_ref[0, :, sl], k_ref[0, :, sl]) * scale
        p = jnp.exp(s - jnp.max(s, axis=1, keepdims=True))
        o_ref[0, :, sl] = _dot(p, v_ref[0, :, sl]) / jnp.sum(p, axis=1, keepdims=True)


def _mem_attend(q, k, v, tq):
    b, t, _ = q.shape
    m = k.shape[1]
    return pl.pallas_call(
        _mem_attn_kernel,
        grid=(b, t // tq),
        in_specs=[pl.BlockSpec((1, tq, SEG), lambda i, j: (i, j, 0)),
                  pl.BlockSpec((1, m, SEG), lambda i, j: (i, 0, 0)),
                  pl.BlockSpec((1, m, SEG), lambda i, j: (i, 0, 0))],
        out_specs=pl.BlockSpec((1, tq, SEG), lambda i, j: (i, j, 0)),
        out_shape=jax.ShapeDtypeStruct((b, t, SEG), F32),
        compiler_params=_cparams(("parallel", "parallel")),
        name="mem_attn",
    )(q, k, v)


def _interleaved_head_mask(rows, cols, heads):
    per_head = rows // heads
    r = lax.broadcasted_iota(jnp.int32, (rows, cols), 0) // per_head
    c = lax.broadcasted_iota(jnp.int32, (rows, cols), 1) % heads
    return r == c


def _mem_sample_kernel(q_ref, k_ref, v_ref, o_ref):
    q8 = q_ref[0]
    qx = jnp.concatenate([q8[:, hh * MEM_DIM:(hh + 1) * MEM_DIM] for hh in range(MEM_HEADS)], axis=0)
    s = _dot_nt(qx, k_ref[...]) * (MEM_DIM ** -0.5)
    s = jnp.where(_interleaved_head_mask(s.shape[0], s.shape[1], MEM_HEADS), s, NEG)
    p = jnp.exp(s - jnp.max(s, axis=1, keepdims=True))
    o = _dot(p, v_ref[...]) / jnp.sum(p, axis=1, keepdims=True)
    for hh in range(MEM_HEADS):
        o_ref[0, :, hh * MEM_DIM:(hh + 1) * MEM_DIM] = o[hh * T_PAD:(hh + 1) * T_PAD, :]


def _mem_sample(q, cache_k, cache_v, layer):
    bd = q.shape[0]
    rows = cache_k.shape[2]
    tok = pl.BlockSpec((1, T_PAD, SEG), lambda i: (i, 0, 0))
    kv = pl.BlockSpec((None, None, rows, MEM_DIM), lambda i: (layer, i, 0, 0))
    return pl.pallas_call(
        _mem_sample_kernel,
        grid=(bd,),
        in_specs=[tok, kv, kv],
        out_specs=tok,
        out_shape=jax.ShapeDtypeStruct((bd, T_PAD, SEG), F32),
        compiler_params=_cparams(("parallel",)),
        name="mem_sample",
    )(q, cache_k, cache_v)


def _flash_init(dv, tq):
    return jnp.full((1, tq), NEG, F32), jnp.zeros((dv + 8, tq), F32)


def _flash_step(st, kn, vtn, qt, bias, mask):
    return _flash_update(st, _dot(kn, qt), vtn, bias, mask)


def _flash_update(st, s, vtn, bias, mask):
    m_old, acc = st
    if bias is not None:
        rows = s.shape[0] // len(bias)
        s = jnp.concatenate([s[i * rows:(i + 1) * rows] + bi for i, bi in enumerate(bias)], axis=0)
    if mask is not None:
        s = jnp.where(mask, s, NEG)
    m_new = jnp.maximum(m_old, jnp.max(s, axis=0, keepdims=True))
    alpha = jnp.exp2(m_old - m_new)
    p = jnp.exp2(s - m_new)
    vt1 = jnp.concatenate([vtn, jnp.ones((8, vtn.shape[1]), F32)], axis=0)
    return m_new, alpha * acc + _dot(vt1, p)


def _flash_finish(st, dv):
    _, acc = st
    return acc[:dv] / acc[dv:dv + 1]


def _moba_prompt_kernel(qt_ref, k_ref, vt_ref, o_ref, kmean_ref, bias_ref, s_ref, *, nb, grp):
    j = pl.program_id(2)
    blk = MOBA_BLOCK
    pair = 2 * HEAD

    @pl.when(j == 0)
    def _():
        for n in range(nb):
            kmean_ref[n:n + 1, :] = jnp.mean(k_ref[0, n * blk:(n + 1) * blk, :], axis=0, keepdims=True)

    qt = qt_ref[0]
    row_head = lax.broadcasted_iota(jnp.int32, (pair, 1), 0) // HEAD
    lane_head = lax.broadcasted_iota(jnp.int32, (1, pair), 1) // HEAD
    qts = [jnp.where(row_head == hh, qt, 0.0) for hh in range(2)]
    valid = lax.broadcasted_iota(jnp.int32, (nb, blk), 0) < j
    for hh in range(2):
        gate = _dot_split(jnp.where(lane_head == hh, kmean_ref[...], 0.0), qt)
        bias_ref[hh] = jnp.where(_top_k_mask(gate, valid, 0), 0.0, NEG)

    def init():
        return _flash_init(HEAD, blk)

    key = lax.broadcasted_iota(jnp.int32, (blk, blk), 0)
    qry = lax.broadcasted_iota(jnp.int32, (blk, blk), 1)
    own = pl.ds(pl.multiple_of(j * blk, blk), blk)
    k_own = k_ref[0, own, :]
    state = tuple(_flash_step(init(), k_own, vt_ref[0, hh, :, own], qts[hh], None, key <= qry)
                  for hh in range(2))

    ngrp = (j + grp - 1) // grp

    def group_rows(g):
        return pl.ds(pl.multiple_of(g * grp * blk, grp * blk), grp * blk)

    def scores(g):
        kn = k_ref[0, group_rows(g), :]
        for hh in range(2):
            s_ref[hh] = _dot(kn, qts[hh])

    scores(0)

    def body(g, st):
        rows = group_rows(g)
        s_cur = [s_ref[hh] for hh in range(2)]
        scores(jnp.minimum(g + 1, ngrp - 1))
        return tuple(_flash_update(st[hh], s_cur[hh], vt_ref[0, hh, :, rows],
                                   [bias_ref[hh, pl.ds(g * grp + i, 1), :] for i in range(grp)], None)
                     for hh in range(2))

    state = lax.fori_loop(0, ngrp, body, state)
    ot = jnp.concatenate([_flash_finish(st, HEAD) for st in state], axis=0)
    o_ref[0] = ot.T


def _moba_prompt(qt, k, vt):
    b, s, _ = k.shape
    nb = s // MOBA_BLOCK
    pair = 2 * HEAD
    grp = math.gcd(nb, KV_GROUP)
    return pl.pallas_call(
        functools.partial(_moba_prompt_kernel, nb=nb, grp=grp),
        grid=(b, SEG // pair, nb),
        in_specs=[pl.BlockSpec((1, pair, MOBA_BLOCK), lambda i, g, j: (i, g, j)),
                  pl.BlockSpec((1, s, pair), lambda i, g, j: (i, 0, g)),
                  pl.BlockSpec((1, 2, HEAD, s), lambda i, g, j: (i, g, 0, 0))],
        out_specs=pl.BlockSpec((1, MOBA_BLOCK, pair), lambda i, g, j: (i, j, g)),
        out_shape=jax.ShapeDtypeStruct((b, s, SEG), F32),
        scratch_shapes=[pltpu.VMEM((nb, pair), F32),
                        pltpu.VMEM((2, nb, MOBA_BLOCK), F32),
                        pltpu.VMEM((2, grp * MOBA_BLOCK, MOBA_BLOCK), F32)],
        compiler_params=_cparams(("parallel", "parallel", "arbitrary")),
        name="moba_prompt",
    )(qt, k, vt)


def _diff_lambda(lam_ref, lam_init):
    lam = lam_ref[...]
    a = jnp.sum(lam[0:1, :] * lam[1:2, :], axis=1, keepdims=True)
    b = jnp.sum(lam[2:3, :] * lam[3:4, :], axis=1, keepdims=True)
    return jnp.exp(a) - jnp.exp(b) + lam_init


def _diff_prompt_kernel(qt_ref, k_ref, vt_ref, lam_ref, g_ref, o_ref, s_ref, *, lam_init, tk):
    j = pl.program_id(2)
    tq = QUERY_TILE
    hw = 2 * HEAD
    row_map = lax.broadcasted_iota(jnp.int32, (hw, 1), 0) // HEAD
    qt2 = []
    for hh in range(2):
        qt = qt_ref[0, hh * hw:(hh + 1) * hw, :]
        qt2.append(jnp.concatenate([jnp.where(row_map == 0, qt, 0.0), jnp.where(row_map == 1, qt, 0.0)], axis=1))

    def tile_rows(n):
        return pl.ds(pl.multiple_of(n * tk, tk), tk)

    def scores(n):
        rows = tile_rows(n)
        for hh in range(2):
            s_ref[hh] = _dot(k_ref[0, rows, hh * hw:(hh + 1) * hw], qt2[hh])

    def update(n, st, s, mask):
        rows = tile_rows(n)
        return tuple(_flash_update(st[hh], s[hh], vt_ref[0, hh * DIFF_VDIM:(hh + 1) * DIFF_VDIM, rows], None, mask)
                     for hh in range(2))

    def body(n, st):
        s_cur = [s_ref[hh] for hh in range(2)]
        scores(n + 1)
        return update(n, st, s_cur, None)

    init = _flash_init(DIFF_VDIM, 2 * tq)
    last = (j * tq) // tk
    scores(0)
    state = lax.fori_loop(0, last, body, (init, init))
    key = last * tk + lax.broadcasted_iota(jnp.int32, (tk, 2 * tq), 0)
    qry = j * tq + lax.broadcasted_iota(jnp.int32, (tk, 2 * tq), 1) % tq
    state = update(last, state, [s_ref[hh] for hh in range(2)], key <= qry)

    lam = _diff_lambda(lam_ref, lam_init)
    for hh in range(2):
        ot = _flash_finish(state[hh], DIFF_VDIM)
        d = (ot[:, :tq] - lam * ot[:, tq:]).T
        o_ref[0, :, hh * DIFF_VDIM:(hh + 1) * DIFF_VDIM] = _rms(d, g_ref[...]) * (1.0 - lam_init)


def _diff_prompt(qt, k, vt, lam, subln, lam_init):
    b, s, _ = k.shape
    pw = 2 * DIFF_VDIM
    tq = QUERY_TILE
    tk = DIFF_KEY_TILE
    return pl.pallas_call(
        functools.partial(_diff_prompt_kernel, lam_init=lam_init, tk=tk),
        grid=(b, DIFF_HEADS // 2, s // tq),
        in_specs=[pl.BlockSpec((1, pw, tq), lambda i, h, j: (i, h, j)),
                  pl.BlockSpec((1, s, pw), lambda i, h, j: (i, 0, h)),
                  pl.BlockSpec((1, pw, s), lambda i, h, j: (i, h, 0)),
                  pl.BlockSpec((4, HEAD), lambda i, h, j: (0, 0)),
                  pl.BlockSpec((1, DIFF_VDIM), lambda i, h, j: (0, 0))],
        out_specs=pl.BlockSpec((1, tq, pw), lambda i, h, j: (i, j, h)),
        out_shape=jax.ShapeDtypeStruct((b, s, SEG), F32),
        scratch_shapes=[pltpu.VMEM((2, tk, 2 * tq), F32)],
        compiler_params=_cparams(("parallel", "parallel", "arbitrary")),
        name="diff_prompt",
    )(qt, k, vt, lam, subln)


def _merge_kernel(x_ref, om_ref, od_ref, oc_ref, nm_ref, wg_ref, bg_ref, wb_ref, wo_ref, y_ref):
    x = x_ref[...]
    d = x.shape[1]
    h = _rms(x, nm_ref[...]).astype(BF16)
    merged = jnp.zeros_like(x)
    for n, o_ref in enumerate((om_ref, od_ref, oc_ref)):
        sl = slice(n * d, (n + 1) * d)
        g = jax.nn.sigmoid(_dot(h, wg_ref[:, sl]) + bg_ref[:, sl])
        merged = merged + g * _dot(o_ref[...].astype(BF16), wb_ref[n])
    y_ref[...] = x + _dot(merged.astype(BF16), wo_ref[...])


def _merge(x2d, o_moba, o_diff, o_mem, norm_mix, w_gate, b_gate, w_branch, w_out, tm):
    n, d = x2d.shape
    tok = lambda i: (i, 0)
    full = lambda i: (0, 0)
    return pl.pallas_call(
        _merge_kernel,
        grid=(n // tm,),
        in_specs=[pl.BlockSpec((tm, d), tok),
                  pl.BlockSpec((tm, SEG), tok),
                  pl.BlockSpec((tm, SEG), tok),
                  pl.BlockSpec((tm, SEG), tok),
                  pl.BlockSpec((1, d), full),
                  pl.BlockSpec((d, 3 * d), full),
                  pl.BlockSpec((1, 3 * d), full),
                  pl.BlockSpec((3, SEG, d), lambda i: (0, 0, 0)),
                  pl.BlockSpec((d, d), full)],
        out_specs=pl.BlockSpec((tm, d), tok),
        out_shape=jax.ShapeDtypeStruct((n, d), F32),
        compiler_params=_cparams(("parallel",)),
        name="merge",
    )(x2d, o_moba, o_diff, o_mem, norm_mix, w_gate, b_gate, w_branch, w_out)


def _ffn_kernel(x_ref, nf_ref, wu_ref, wd_ref, y_ref, *, n_chunks):
    x = x_ref[...]
    hn = _rms(x, nf_ref[...]).astype(BF16)
    cw = wu_ref.shape[1] // n_chunks
    acc = x
    for c in range(n_chunks):
        u = jnp.square(jnp.maximum(_dot(hn, wu_ref[:, c * cw:(c + 1) * cw]), 0.0))
        acc = acc + _dot(u.astype(BF16), wd_ref[c * cw:(c + 1) * cw, :])
    y_ref[...] = acc


def _ffn(x2d, norm_ffn, w_up, w_down, tm):
    n, d = x2d.shape
    dff = w_up.shape[1]
    tok = lambda i: (i, 0)
    full = lambda i: (0, 0)
    return pl.pallas_call(
        functools.partial(_ffn_kernel, n_chunks=4),
        grid=(n // tm,),
        in_specs=[pl.BlockSpec((tm, d), tok),
                  pl.BlockSpec((1, d), full),
                  pl.BlockSpec((d, dff), full),
                  pl.BlockSpec((dff, d), full)],
        out_specs=pl.BlockSpec((tm, d), tok),
        out_shape=jax.ShapeDtypeStruct((n, d), F32),
        compiler_params=_cparams(("parallel",)),
        name="ffn",
    )(x2d, norm_ffn, w_up, w_down)


def _page_specs(layer, npg, block):
    def spec(jj):
        return pl.BlockSpec((None, None) + block,
                            lambda b, c, pt: (layer, pt[b, c * npg + jj]) + (0,) * len(block))
    return [spec(jj) for jj in range(npg)]


def _new_token_scores(qrows, knx, t_dec):
    tok = lax.broadcasted_iota(jnp.int32, (qrows.shape[0], 1), 0) % T_PAD
    return [jnp.where(tok >= t, jnp.sum(qrows * knx[t], axis=1, keepdims=True), NEG) for t in range(t_dec)]


def _diff_sample_kernel(pt_ref, q_ref, kn_ref, vn_ref, lam_ref, g_ref, *rest, npg, lam_init, t_dec):
    del pt_ref
    k_refs, v_refs = rest[:npg], rest[npg:2 * npg]
    o_ref, m_ref, l_ref, acc_ref = rest[2 * npg:]
    c = pl.program_id(1)
    nrow = DIFF_HEADS * 2 * T_PAD
    hw = 2 * HEAD

    def head_rows(x8, masked):
        lane_map = lax.broadcasted_iota(jnp.int32, (1, hw), 1) // HEAD
        out = []
        for hh in range(DIFF_HEADS):
            xs = x8[:, hh * hw:(hh + 1) * hw]
            for mp in range(2):
                out.append(jnp.where(lane_map == mp, xs, 0.0) if masked else xs)
        return jnp.concatenate(out, axis=0)

    qx = head_rows(q_ref[0] * (HEAD ** -0.5), True)

    @pl.when(c == 0)
    def _():
        m_ref[...] = jnp.full(m_ref.shape, NEG, F32)
        l_ref[...] = jnp.zeros(l_ref.shape, F32)
        acc_ref[...] = jnp.zeros(acc_ref.shape, F32)

    s = jnp.concatenate([_dot_nt(qx, k_refs[p][...]) for p in range(npg)], axis=1)
    row_head = lax.broadcasted_iota(jnp.int32, s.shape, 0) // (2 * T_PAD)
    col_head = lax.broadcasted_iota(jnp.int32, s.shape, 1) % DIFF_HEADS
    s = jnp.where(row_head == col_head, s, NEG)
    m_old = m_ref[...]
    m_new = jnp.maximum(m_old, jnp.max(s, axis=1, keepdims=True))
    alpha = jnp.exp(m_old - m_new)
    p_all = jnp.exp(s - m_new)
    pw = k_refs[0].shape[0]
    pv = _dot(p_all[:, :pw], v_refs[0][...])
    for p in range(1, npg):
        pv = pv + _dot(p_all[:, p * pw:(p + 1) * pw], v_refs[p][...])
    l_ref[...] = alpha * l_ref[...] + jnp.sum(p_all, axis=1, keepdims=True)
    acc_ref[...] = alpha * acc_ref[...] + pv
    m_ref[...] = m_new

    @pl.when(c == pl.num_programs(1) - 1)
    def _():
        kn, vn = kn_ref[0], vn_ref[0]
        knx = [head_rows(jnp.broadcast_to(kn[t:t + 1, :], (T_PAD, SEG)), False) for t in range(t_dec)]
        vnx = [head_rows(jnp.broadcast_to(vn[t:t + 1, :], (T_PAD, SEG)), False) for t in range(t_dec)]
        sn = _new_token_scores(qx, knx, t_dec)
        m0 = m_ref[...]
        m1 = m0
        for t in range(t_dec):
            m1 = jnp.maximum(m1, sn[t])
        a = jnp.exp(m0 - m1)
        l = a * l_ref[...]
        acc = a * acc_ref[...]
        for t in range(t_dec):
            pt = jnp.exp(sn[t] - m1)
            l = l + pt
            acc = acc + pt * vnx[t]
        o = acc / l
        lam = _diff_lambda(lam_ref, lam_init)
        for hh in range(DIFF_HEADS):
            r0 = 2 * hh * T_PAD
            d = o[r0:r0 + T_PAD, :] - lam * o[r0 + T_PAD:r0 + 2 * T_PAD, :]
            o_ref[0, :, hh * DIFF_VDIM:(hh + 1) * DIFF_VDIM] = _rms(d, g_ref[...]) * (1.0 - lam_init)


def _diff_sample(page_table, q, kn, vn, lam, subln, cache_k, cache_v, layer, lam_init, t_dec):
    bd = q.shape[0]
    n_pages = page_table.shape[1]
    rows = cache_k.shape[2]
    npg = min(PAGES_PER_STEP, n_pages)
    nrow = DIFF_HEADS * 2 * T_PAD
    tokspec = pl.BlockSpec((1, T_PAD, SEG), lambda b, c, pt: (b, 0, 0))
    grid_spec = pltpu.PrefetchScalarGridSpec(
        num_scalar_prefetch=1,
        grid=(bd, n_pages // npg),
        in_specs=[tokspec, tokspec, tokspec,
                  pl.BlockSpec((4, HEAD), lambda b, c, pt: (0, 0)),
                  pl.BlockSpec((1, DIFF_VDIM), lambda b, c, pt: (0, 0))]
                 + _page_specs(layer, npg, (rows, 128)) + _page_specs(layer, npg, (rows, 128)),
        out_specs=tokspec,
        scratch_shapes=[pltpu.VMEM((nrow, 1), F32),
                        pltpu.VMEM((nrow, 1), F32),
                        pltpu.VMEM((nrow, DIFF_VDIM), F32)])
    return pl.pallas_call(
        functools.partial(_diff_sample_kernel, npg=npg, lam_init=lam_init, t_dec=t_dec),
        grid_spec=grid_spec,
        out_shape=jax.ShapeDtypeStruct((bd, T_PAD, SEG), F32),
        compiler_params=_cparams(("parallel", "arbitrary")),
        name="diff_sample",
    )(page_table, q, kn, vn, lam, subln, *([cache_k] * npg), *([cache_v] * npg))


def _group_query(q8):
    rows = lax.broadcasted_iota(jnp.int32, (MOBA_HEADS * T_PAD, SEG), 0) // T_PAD
    lanes = lax.broadcasted_iota(jnp.int32, (MOBA_HEADS * T_PAD, SEG), 1) // HEAD
    return jnp.where(rows == lanes, jnp.concatenate([q8] * MOBA_HEADS, axis=0), 0.0)


def _moba_sample_kernel(pt_ref, q_ref, kn_ref, vn_ref, *rest, npg, t_dec, nblk):
    del pt_ref
    k_refs, v_refs = rest[:npg], rest[npg:2 * npg]
    o_ref, ks_ref, pm_ref, pl_ref, po_ref = rest[2 * npg:]
    c = pl.program_id(1)
    qg = _group_query(q_ref[0] * (HEAD ** -0.5))
    pg = k_refs[0].shape[2]
    ppb = MOBA_BLOCK // pg
    bpc = npg // ppb
    lane = lax.broadcasted_iota(jnp.int32, (1, 128), 1)

    @pl.when(c == 0)
    def _():
        ks_ref[...] = jnp.zeros(ks_ref.shape, F32)
        pm_ref[...] = jnp.zeros(pm_ref.shape, F32)
        pl_ref[...] = jnp.zeros(pl_ref.shape, F32)

    for bb in range(bpc):
        kts = [k_refs[bb * ppb + i][...].reshape(SEG, pg) for i in range(ppb)]
        vts = [v_refs[bb * ppb + i][...].reshape(SEG, pg) for i in range(ppb)]
        s = jnp.concatenate([_dot(qg, kt) for kt in kts], axis=1)
        m = jnp.max(s, axis=1, keepdims=True)
        p = jnp.exp(s - m)
        l = jnp.sum(p, axis=1, keepdims=True)
        o = _dot_nt(p[:, :pg], vts[0])
        ksum = kts[0]
        for i in range(1, ppb):
            o = o + _dot_nt(p[:, i * pg:(i + 1) * pg], vts[i])
            ksum = ksum + kts[i]
        ksum = jnp.sum(ksum, axis=1, keepdims=True)
        n = c * bpc + bb
        ks_ref[...] = jnp.where(lane == n, ksum, ks_ref[...])
        pm_ref[...] = jnp.where(lane == n, m, pm_ref[...])
        pl_ref[...] = jnp.where(lane == n, l, pl_ref[...])
        po_ref[n] = o

    @pl.when(c == pl.num_programs(1) - 1)
    def _():
        gate = _dot_split(qg, ks_ref[...])
        sel = _top_k_mask(gate, lane < nblk, 1)
        kn, vn = kn_ref[0], vn_ref[0]
        sn = _new_token_scores(qg, [kn[t:t + 1, :] for t in range(t_dec)], t_dec)
        pm = jnp.where(sel, pm_ref[...], NEG)
        mt = jnp.max(pm, axis=1, keepdims=True)
        for t in range(t_dec):
            mt = jnp.maximum(mt, sn[t])
        w = jnp.where(sel, jnp.exp(pm - mt), 0.0)
        l = jnp.sum(w * pl_ref[...], axis=1, keepdims=True)
        o = jnp.zeros((MOBA_HEADS * T_PAD, SEG), F32)
        for n in range(nblk):
            o = o + w[:, n:n + 1] * po_ref[n]
        for t in range(t_dec):
            pt = jnp.exp(sn[t] - mt)
            l = l + pt
            o = o + pt * vn[t:t + 1, :]
        o = o / l
        lane_head = lax.broadcasted_iota(jnp.int32, (T_PAD, SEG), 1) // HEAD
        out = jnp.zeros((T_PAD, SEG), F32)
        for hh in range(MOBA_HEADS):
            out = jnp.where(lane_head == hh, o[hh * T_PAD:(hh + 1) * T_PAD, :], out)
        o_ref[0] = out


def _moba_sample(page_table, q, kn, vn, cache_kt, cache_vt, layer, t_dec):
    bd = q.shape[0]
    n_pages = page_table.shape[1]
    page = cache_kt.shape[4]
    npg = min(PAGES_PER_STEP, n_pages)
    nblk = n_pages * page // MOBA_BLOCK
    nrow = MOBA_HEADS * T_PAD
    tokspec = pl.BlockSpec((1, T_PAD, SEG), lambda b, c, pt: (b, 0, 0))
    blk = (MOBA_HEADS, HEAD, page)
    grid_spec = pltpu.PrefetchScalarGridSpec(
        num_scalar_prefetch=1,
        grid=(bd, n_pages // npg),
        in_specs=[tokspec, tokspec, tokspec] + _page_specs(layer, npg, blk) + _page_specs(layer, npg, blk),
        out_specs=tokspec,
        scratch_shapes=[pltpu.VMEM((SEG, 128), F32),
                        pltpu.VMEM((nrow, 128), F32),
                        pltpu.VMEM((nrow, 128), F32),
                        pltpu.VMEM((nblk, nrow, SEG), F32)])
    return pl.pallas_call(
        functools.partial(_moba_sample_kernel, npg=npg, t_dec=t_dec, nblk=nblk),
        grid_spec=grid_spec,
        out_shape=jax.ShapeDtypeStruct((bd, T_PAD, SEG), F32),
        compiler_params=_cparams(("parallel", "arbitrary")),
        name="moba_sample",
    )(page_table, q, kn, vn, *([cache_kt] * npg), *([cache_vt] * npg))


def _rope_tables(pos):
    half = HEAD // 2
    inv = ROPE_THETA ** (-2.0 * jnp.arange(half, dtype=F32) / HEAD)
    ang = pos.astype(F32)[:, None] * inv[None, :]
    cos, sin = jnp.cos(ang), jnp.sin(ang)
    return jnp.concatenate([cos] * 4, axis=1), jnp.concatenate([-sin, sin] * 2, axis=1)


def _block_diag(width, group, value, dtype):
    i = jnp.arange(width) // group
    return jnp.where(i[:, None] == i[None, :], value, 0.0).astype(dtype)


def kernel(x_prompt, x_sample, cache_moba_k, cache_moba_v, cache_diff_k, cache_diff_v, cache_mem_k, cache_mem_v, page_table, mem_prompt, norm_mix, w_in, moba_qn, moba_kn, diff_qn, diff_kn, diff_lambda, diff_subln, mem_norm, w_mem_kv, mem_qn, mem_kn, w_branch, w_gate, b_gate, w_out, norm_ffn, w_up, w_down):
    depth = w_in.shape[0]
    b, s, d = x_prompt.shape
    bd, t_dec, _ = x_sample.shape
    n_pool, page = cache_moba_k.shape[1], cache_moba_k.shape[2]
    n_pages = page_table.shape[1]
    past = n_pages * page
    assert d == 2 * SEG and w_in.shape[2] == 7 * SEG
    assert s % TOKEN_TILE == 0 and past % MOBA_BLOCK == 0 and MOBA_BLOCK % page == 0 and page % 128 == 0
    assert t_dec <= T_PAD and n_pages % min(PAGES_PER_STEP, n_pages) == 0
    assert MOBA_TOPK <= past // MOBA_BLOCK <= 128

    tm = TOKEN_TILE
    cos_p, sin_p = _rope_tables(jnp.arange(s, dtype=jnp.int32))
    cos_s, sin_s = _rope_tables(past + jnp.arange(T_PAD, dtype=jnp.int32))
    cos_s, sin_s = jnp.tile(cos_s, (bd, 1)), jnp.tile(sin_s, (bd, 1))
    gmat = _block_diag(SEG, HEAD, 1.0 / HEAD, F32)

    w_in_b, w_mem_b = w_in.astype(BF16), w_mem_kv.astype(BF16)
    w_gate_b, w_branch_b, w_out_b = w_gate.astype(BF16), w_branch.astype(BF16), w_out.astype(BF16)
    w_up_b, w_down_b = w_up.astype(BF16), w_down.astype(BF16)

    c_mkt = jnp.transpose(cache_moba_k, (0, 1, 3, 4, 2))
    c_mvt = jnp.transpose(cache_moba_v, (0, 1, 3, 4, 2))
    c_dk = cache_diff_k.reshape(depth, n_pool, page * DIFF_HEADS, 2 * HEAD)
    c_dv = cache_diff_v.reshape(depth, n_pool, page * DIFF_HEADS, DIFF_VDIM)
    c_ck = cache_mem_k.reshape(depth, bd, -1, MEM_DIM)
    c_cv = cache_mem_v.reshape(depth, bd, -1, MEM_DIM)

    xp = x_prompt.reshape(b * s, d)
    xs = jnp.pad(x_sample, ((0, 0), (0, T_PAD - t_dec), (0, 0))).reshape(bd * T_PAD, d)
    ns = bd * T_PAD
    outs = [[] for _ in range(10)]
    for l in range(depth):
        lam_init = 0.8 - 0.6 * math.exp(-0.3 * l)
        gains = jnp.concatenate([jnp.tile(moba_qn[l], 8)[None], jnp.tile(moba_kn[l], 8)[None],
                                 jnp.tile(diff_qn[l], 8)[None], jnp.tile(diff_kn[l], 8)[None],
                                 jnp.tile(mem_qn[l], 4)[None], jnp.zeros((3, SEG), F32)], axis=0)
        nm, nf = norm_mix[l][None], norm_ffn[l][None]
        subln = diff_subln[l][None]

        mk, mkt, mvt, mqt, dk, dv, dqt, dvt, cq = _project_prompt(
            xp, b, s, nm, w_in_b[l], gains, cos_p, sin_p, gmat)
        o_moba = _moba_prompt(mqt, mk, mvt)
        o_diff = _diff_prompt(dqt, dk, dvt, diff_lambda[l], subln, lam_init)
        m_k, m_v = _mem_kv(mem_prompt, mem_norm[l][None], w_mem_b[l], mem_kn[l][None])
        o_mem = _mem_attend(cq, m_k, m_v, 1024 if s % 1024 == 0 else TOKEN_TILE)
        xp = _merge(xp, o_moba.reshape(b * s, SEG), o_diff.reshape(b * s, SEG), o_mem.reshape(b * s, SEG),
                    nm, w_gate_b[l], b_gate[l][None], w_branch_b[l], w_out_b[l], tm)
        xp = _ffn(xp, nf, w_up_b[l], w_down_b[l], tm)
        for lst, a in zip(outs[:6], (mkt, mvt, dk, dv, m_k, m_v)):
            lst.append(a)

        smq, smk, smv, sdq, sdk, sdv, scq = _project(xs, nm, w_in_b[l], gains, cos_s, sin_s, gmat)
        t3 = lambda a: a.reshape(bd, T_PAD, SEG)
        so_moba = _moba_sample(page_table, t3(smq), t3(smk), t3(smv), c_mkt, c_mvt, l, t_dec)
        so_diff = _diff_sample(page_table, t3(sdq), t3(sdk), t3(sdv), diff_lambda[l], subln,
                               c_dk, c_dv, l, lam_init, t_dec)
        so_mem = _mem_sample(t3(scq), c_ck, c_cv, l)
        xs = _merge(xs, so_moba.reshape(ns, SEG), so_diff.reshape(ns, SEG), so_mem.reshape(ns, SEG),
                    nm, w_gate_b[l], b_gate[l][None], w_branch_b[l], w_out_b[l], ns)
        xs = _ffn(xs, nf, w_up_b[l], w_down_b[l], ns)
        for lst, a in zip(outs[6:], (smk, smv, sdk, sdv)):
            lst.append(t3(a)[:, :t_dec])

    st = lambda lst, shape: jnp.stack(lst).reshape((depth,) + shape)
    tr = lambda lst: jnp.transpose(jnp.stack(lst), (0, 1, 4, 2, 3))
    return (xp.reshape(b, s, d),
            xs.reshape(bd, T_PAD, d)[:, :t_dec],
            tr(outs[0]), tr(outs[1]),
            st(outs[2], (b, s, DIFF_HEADS, 2 * HEAD)), st(outs[3], (b, s, DIFF_HEADS, DIFF_VDIM)),
            st(outs[4], (b, -1, MEM_HEADS, MEM_DIM)), st(outs[5], (b, -1, MEM_HEADS, MEM_DIM)),
            st(outs[6], (bd, t_dec, 8, HEAD)), st(outs[7], (bd, t_dec, 8, HEAD)),
            st(outs[8], (bd, t_dec, DIFF_HEADS, 2 * HEAD)), st(outs[9], (bd, t_dec, DIFF_HEADS, DIFF_VDIM)))
```

```python
import functools
import math

import jax
import jax.numpy as jnp
from jax import lax
from jax.experimental import pallas as pl
from jax.experimental.pallas import tpu as pltpu

F32 = jnp.float32
BF16 = jnp.bfloat16

NEG = -1e30
NORM_EPS = 1e-6
ROPE_THETA = 10000.0
SEG = 512
HEAD = 64
Q_SCALE = HEAD ** -0.5 * math.log2(math.e)
MOBA_HEADS = 8
MOBA_BLOCK = 256
MOBA_TOPK = 3
MEM_HEADS = 4
MEM_DIM = 128
DIFF_HEADS = 4
DIFF_VDIM = 128
T_PAD = 8
PAGES_PER_STEP = 16
TOKEN_TILE = 512
QUERY_TILE = 256
KV_GROUP = 4
DIFF_KEY_TILE = 512
VMEM_LIMIT = 56 * 1024 * 1024


def _cparams(sem):
    return pltpu.CompilerParams(dimension_semantics=sem, vmem_limit_bytes=VMEM_LIMIT)


def _rms(x, g):
    ms = jnp.mean(x * x, axis=-1, keepdims=True)
    return x * lax.rsqrt(ms + NORM_EPS) * g


def _dot(a, b):
    return jnp.dot(a, b, preferred_element_type=F32)


def _dot_nt(a, b):
    return lax.dot_general(a, b, (((1,), (1,)), ((), ())), preferred_element_type=F32)


def _split(a):
    hi = a.astype(BF16)
    lo = (a - hi.astype(F32)).astype(BF16)
    return hi, lo


def _dot_split(a, b):
    ah, al = _split(a)
    bh, bl = _split(b)
    return _dot(ah, bh) + _dot(al, bh) + _dot(ah, bl)


def _top_k_mask(gate, valid, axis):
    g = jnp.where(valid, gate, NEG)
    iota = lax.broadcasted_iota(jnp.int32, g.shape, axis).astype(F32)
    sel = jnp.zeros(g.shape, dtype=jnp.bool_)
    for _ in range(MOBA_TOPK):
        mx = jnp.max(g, axis=axis, keepdims=True)
        idx = jnp.min(jnp.where(g == mx, iota, 1e9), axis=axis, keepdims=True)
        pick = iota == idx
        sel = jnp.logical_or(sel, pick)
        g = jnp.where(pick, -jnp.inf, g)
    return jnp.logical_and(sel, valid)


def _proj_body(x_ref, nm_ref, w_ref, gains_ref, cos_ref, sin_ref, gmat_ref):
    x = x_ref[...]
    h = _rms(x, nm_ref[...]).astype(BF16)
    cos = jnp.concatenate([cos_ref[...]] * 4, axis=1)
    sin = jnp.concatenate([sin_ref[...]] * 4, axis=1)
    lane = lax.broadcasted_iota(jnp.int32, (1, SEG), 1)
    first_half = (lane % HEAD) < (HEAD // 2)

    def seg(i):
        return _dot(h, w_ref[:, i * SEG:(i + 1) * SEG])

    def norm_rope(z, gain):
        ms = _dot(z * z, gmat_ref[...])
        zn = z * lax.rsqrt(ms + NORM_EPS) * gain
        swapped = jnp.where(first_half, pltpu.roll(zn, SEG - HEAD // 2, 1),
                            pltpu.roll(zn, HEAD // 2, 1))
        return zn * cos + swapped * sin

    mq = norm_rope(seg(0), gains_ref[0:1, :])
    mk = norm_rope(seg(1), gains_ref[1:2, :])
    mv = seg(2)
    dq = norm_rope(seg(3), gains_ref[2:3, :])
    dk = norm_rope(seg(4), gains_ref[3:4, :])
    dv = seg(5)
    zc = seg(6)
    cq = jnp.concatenate(
        [_rms(zc[:, hh * MEM_DIM:(hh + 1) * MEM_DIM], gains_ref[4:5, hh * MEM_DIM:(hh + 1) * MEM_DIM])
         for hh in range(MEM_HEADS)], axis=1)
    return mq, mk, mv, dq, dk, dv, cq


def _proj_kernel(x_ref, nm_ref, w_ref, gains_ref, cos_ref, sin_ref, gmat_ref,
                 mq_ref, mk_ref, mv_ref, dq_ref, dk_ref, dv_ref, cq_ref):
    outs = _proj_body(x_ref, nm_ref, w_ref, gains_ref, cos_ref, sin_ref, gmat_ref)
    for ref, val in zip((mq_ref, mk_ref, mv_ref, dq_ref, dk_ref, dv_ref, cq_ref), outs):
        ref[...] = val


def _proj_prompt_kernel(*refs):
    mk_ref, mkt_ref, mvt_ref, mqt_ref, dk_ref, dv_ref, dqt_ref, dvt_ref, cq_ref = refs[-9:]
    mq, mk, mv, dq, dk, dv, cq = _proj_body(*refs[:7])
    tm = mq.shape[0]
    mk_ref[0] = mk
    mkt_ref[0] = mk.T.reshape(MOBA_HEADS, HEAD, tm)
    mvt_ref[0] = mv.T.reshape(MOBA_HEADS, HEAD, tm)
    mqt_ref[0] = (mq * Q_SCALE).T
    dk_ref[0] = dk
    dv_ref[0] = dv
    dqt_ref[0] = (dq * Q_SCALE).T
    dvt_ref[0] = dv.T
    cq_ref[0] = cq


def _proj_in_specs(tm, d, d_in, n_pos_tiles):
    full = lambda i: (0, 0)
    pos = lambda i: (i % n_pos_tiles, 0)
    return [pl.BlockSpec((tm, d), lambda i: (i, 0)),
            pl.BlockSpec((1, d), full),
            pl.BlockSpec((d, d_in), full),
            pl.BlockSpec((8, SEG), full),
            pl.BlockSpec((tm, 128), pos),
            pl.BlockSpec((tm, 128), pos),
            pl.BlockSpec((SEG, SEG), full)]


def _project(x2d, norm_mix, w_in_bf16, gains, cos_t, sin_t, gmat):
    n, d = x2d.shape
    out = jax.ShapeDtypeStruct((n, SEG), F32)
    return pl.pallas_call(
        _proj_kernel,
        grid=(1,),
        in_specs=_proj_in_specs(n, d, w_in_bf16.shape[1], 1),
        out_specs=[pl.BlockSpec((n, SEG), lambda i: (i, 0))] * 7,
        out_shape=[out] * 7,
        compiler_params=_cparams(("parallel",)),
        name="proj",
    )(x2d, norm_mix, w_in_bf16, gains, cos_t, sin_t, gmat)


def _project_prompt(x2d, b, s, depth, layer, prev, norm_mix, w_in_bf16, gains, cos_t, sin_t, gmat):
    n, d = x2d.shape
    tm = TOKEN_TILE
    npt = s // tm
    std = jax.ShapeDtypeStruct((b, s, SEG), F32)
    tr = jax.ShapeDtypeStruct((b, SEG, s), F32)
    std_all = jax.ShapeDtypeStruct((depth, b, s, SEG), F32)
    hd_all = jax.ShapeDtypeStruct((depth, b, MOBA_HEADS, HEAD, s), F32)
    std_spec = pl.BlockSpec((1, tm, SEG), lambda i: (i // npt, i % npt, 0))
    tr_spec = pl.BlockSpec((1, SEG, tm), lambda i: (i // npt, 0, i % npt))
    std_all_spec = pl.BlockSpec((None, 1, tm, SEG), lambda i: (layer, i // npt, i % npt, 0))
    hd_all_spec = pl.BlockSpec((None, 1, MOBA_HEADS, HEAD, tm), lambda i: (layer, i // npt, 0, 0, i % npt))
    n_in = 7
    return pl.pallas_call(
        _proj_prompt_kernel,
        grid=(n // tm,),
        in_specs=_proj_in_specs(tm, d, w_in_bf16.shape[1], npt) + [pl.BlockSpec(memory_space=pl.ANY)] * len(prev),
        out_specs=[std_spec, hd_all_spec, hd_all_spec, tr_spec, std_all_spec, std_all_spec, tr_spec, tr_spec,
                   std_spec],
        out_shape=[std, hd_all, hd_all, tr, std_all, std_all, tr, tr, std],
        input_output_aliases=({n_in: 1, n_in + 1: 2, n_in + 2: 4, n_in + 3: 5} if prev else {}),
        compiler_params=_cparams(("parallel",)),
        name="proj_prompt",
    )(x2d, norm_mix, w_in_bf16, gains, cos_t, sin_t, gmat, *prev)


def _memkv_kernel(mem_ref, mn_ref, w_ref, kn_ref, k_ref, v_ref):
    m = _rms(mem_ref[0], mn_ref[...]).astype(BF16)
    z = _dot(m, w_ref[...])
    for hh in range(MEM_HEADS):
        sl = slice(hh * MEM_DIM, (hh + 1) * MEM_DIM)
        k_ref[0, :, sl] = _rms(z[:, sl], kn_ref[...])
    v_ref[0] = z[:, SEG:]


def _mem_kv(mem, mem_norm, w_mem_kv_bf16, mem_kn):
    b, m, d = mem.shape
    out = jax.ShapeDtypeStruct((b, m, SEG), F32)
    return pl.pallas_call(
        _memkv_kernel,
        grid=(b,),
        in_specs=[pl.BlockSpec((1, m, d), lambda i: (i, 0, 0)),
                  pl.BlockSpec((1, d), lambda i: (0, 0)),
                  pl.BlockSpec((d, 2 * SEG), lambda i: (0, 0)),
                  pl.BlockSpec((1, MEM_DIM), lambda i: (0, 0))],
        out_specs=[pl.BlockSpec((1, m, SEG), lambda i: (i, 0, 0))] * 2,
        out_shape=[out, out],
        compiler_params=_cparams(("parallel",)),
        name="mem_kv",
    )(mem, mem_norm, w_mem_kv_bf16, mem_kn)


def _mem_attn_kernel(q_ref, k_ref, v_ref, o_ref):
    scale = MEM_DIM ** -0.5
    for hh in range(MEM_HEADS):
        sl = slice(hh * MEM_DIM, (hh + 1) * MEM_DIM)
        s = _dot_nt(q_ref[0, :, sl], k_ref[0, :, sl]) * scale
        p = jnp.exp(s - jnp.max(s, axis=1, keepdims=True))
        o_ref[0, :, sl] = _dot(p, v_ref[0, :, sl]) / jnp.sum(p, axis=1, keepdims=True)


def _mem_attend(q, k, v, tq):
    b, t, _ = q.shape
    m = k.shape[1]
    return pl.pallas_call(
        _mem_attn_kernel,
        grid=(b, t // tq),
        in_specs=[pl.BlockSpec((1, tq, SEG), lambda i, j: (i, j, 0)),
                  pl.BlockSpec((1, m, SEG), lambda i, j: (i, 0, 0)),
                  pl.BlockSpec((1, m, SEG), lambda i, j: (i, 0, 0))],
        out_specs=pl.BlockSpec((1, tq, SEG), lambda i, j: (i, j, 0)),
        out_shape=jax.ShapeDtypeStruct((b, t, SEG), F32),
        compiler_params=_cparams(("parallel", "parallel")),
        name="mem_attn",
    )(q, k, v)


def _interleaved_head_mask(rows, cols, heads):
    per_head = rows // heads
    r = lax.broadcasted_iota(jnp.int32, (rows, cols), 0) // per_head
    c = lax.broadcasted_iota(jnp.int32, (rows, cols), 1) % heads
    return r == c


def _mem_sample_kernel(q_ref, k_ref, v_ref, o_ref):
    q8 = q_ref[0]
    qx = jnp.concatenate([q8[:, hh * MEM_DIM:(hh + 1) * MEM_DIM] for hh in range(MEM_HEADS)], axis=0)
    s = _dot_nt(qx, k_ref[...]) * (MEM_DIM ** -0.5)
    s = jnp.where(_interleaved_head_mask(s.shape[0], s.shape[1], MEM_HEADS), s, NEG)
    p = jnp.exp(s - jnp.max(s, axis=1, keepdims=True))
    o = _dot(p, v_ref[...]) / jnp.sum(p, axis=1, keepdims=True)
    for hh in range(MEM_HEADS):
        o_ref[0, :, hh * MEM_DIM:(hh + 1) * MEM_DIM] = o[hh * T_PAD:(hh + 1) * T_PAD, :]


def _mem_sample(q, cache_k, cache_v, layer):
    bd = q.shape[0]
    rows = cache_k.shape[2]
    tok = pl.BlockSpec((1, T_PAD, SEG), lambda i: (i, 0, 0))
    kv = pl.BlockSpec((None, None, rows, MEM_DIM), lambda i: (layer, i, 0, 0))
    return pl.pallas_call(
        _mem_sample_kernel,
        grid=(bd,),
        in_specs=[tok, kv, kv],
        out_specs=tok,
        out_shape=jax.ShapeDtypeStruct((bd, T_PAD, SEG), F32),
        compiler_params=_cparams(("parallel",)),
        name="mem_sample",
    )(q, cache_k, cache_v)


def _flash_init(dv, tq):
    return jnp.full((1, tq), NEG, F32), jnp.zeros((dv + 8, tq), F32)


def _flash_step(st, kn, vtn, qt, bias, mask):
    return _flash_update(st, _dot(kn, qt), vtn, bias, mask)


def _flash_update(st, s, vtn, bias, mask):
    m_old, acc = st
    if bias is not None:
        rows = s.shape[0] // len(bias)
        s = jnp.concatenate([s[i * rows:(i + 1) * rows] + bi for i, bi in enumerate(bias)], axis=0)
    if mask is not None:
        s = jnp.where(mask, s, NEG)
    m_new = jnp.maximum(m_old, jnp.max(s, axis=0, keepdims=True))
    alpha = jnp.exp2(m_old - m_new)
    p = jnp.exp2(s - m_new)
    vt1 = jnp.concatenate([vtn, jnp.ones((8, vtn.shape[1]), F32)], axis=0)
    return m_new, alpha * acc + _dot(vt1, p)


def _flash_finish(st, dv):
    _, acc = st
    return acc[:dv] / acc[dv:dv + 1]


def _moba_prompt_kernel(qt_ref, k_ref, vt_ref, o_ref, kmean_ref, bias_ref, s_ref, *, nb, grp):
    j = pl.program_id(2)
    blk = MOBA_BLOCK
    pair = 2 * HEAD

    @pl.when(j == 0)
    def _():
        for n in range(nb):
            kmean_ref[n:n + 1, :] = jnp.mean(k_ref[0, n * blk:(n + 1) * blk, :], axis=0, keepdims=True)

    qt = qt_ref[0]
    row_head = lax.broadcasted_iota(jnp.int32, (pair, 1), 0) // HEAD
    lane_head = lax.broadcasted_iota(jnp.int32, (1, pair), 1) // HEAD
    qts = [jnp.where(row_head == hh, qt, 0.0) for hh in range(2)]
    valid = lax.broadcasted_iota(jnp.int32, (nb, blk), 0) < j
    for hh in range(2):
        gate = _dot_split(jnp.where(lane_head == hh, kmean_ref[...], 0.0), qt)
        bias_ref[:, hh * blk:(hh + 1) * blk] = jnp.where(_top_k_mask(gate, valid, 0), 0.0, NEG)

    qt2 = jnp.concatenate(qts, axis=1)

    def vt_rows(rows):
        return vt_ref[0, :, :, rows].reshape(pair, -1)

    key = lax.broadcasted_iota(jnp.int32, (blk, 2 * blk), 0)
    qry = lax.broadcasted_iota(jnp.int32, (blk, 2 * blk), 1) % blk
    own = pl.ds(pl.multiple_of(j * blk, blk), blk)
    state = _flash_step(_flash_init(pair, 2 * blk), k_ref[0, own, :], vt_rows(own), qt2, None, key <= qry)

    ngrp = (j + grp - 1) // grp

    def group_rows(g):
        return pl.ds(pl.multiple_of(g * grp * blk, grp * blk), grp * blk)

    def scores(g):
        s_ref[...] = _dot(k_ref[0, group_rows(g), :], qt2)

    scores(0)

    def body(g, st):
        s_cur = s_ref[...]
        scores(jnp.minimum(g + 1, ngrp - 1))
        return _flash_update(st, s_cur, vt_rows(group_rows(g)),
                             [bias_ref[pl.ds(g * grp + i, 1), :] for i in range(grp)], None)

    _, acc = lax.fori_loop(0, ngrp, body, state)
    ot = jnp.concatenate([acc[hh * HEAD:(hh + 1) * HEAD, hh * blk:(hh + 1) * blk]
                          / acc[pair:pair + 1, hh * blk:(hh + 1) * blk] for hh in range(2)], axis=0)
    o_ref[0] = ot.T


def _moba_prompt(qt, k, vt, layer):
    b, s, _ = k.shape
    nb = s // MOBA_BLOCK
    pair = 2 * HEAD
    grp = math.gcd(nb, KV_GROUP)
    return pl.pallas_call(
        functools.partial(_moba_prompt_kernel, nb=nb, grp=grp),
        grid=(b, SEG // pair, nb),
        in_specs=[pl.BlockSpec((1, pair, MOBA_BLOCK), lambda i, g, j: (i, g, j)),
                  pl.BlockSpec((1, s, pair), lambda i, g, j: (i, 0, g)),
                  pl.BlockSpec((None, 1, 2, HEAD, s), lambda i, g, j: (layer, i, g, 0, 0))],
        out_specs=pl.BlockSpec((1, MOBA_BLOCK, pair), lambda i, g, j: (i, j, g)),
        out_shape=jax.ShapeDtypeStruct((b, s, SEG), F32),
        scratch_shapes=[pltpu.VMEM((nb, pair), F32),
                        pltpu.VMEM((nb, 2 * MOBA_BLOCK), F32),
                        pltpu.VMEM((grp * MOBA_BLOCK, 2 * MOBA_BLOCK), F32)],
        compiler_params=_cparams(("parallel", "parallel", "arbitrary")),
        name="moba_prompt",
    )(qt, k, vt)


def _diff_lambda(lam_ref, lam_init):
    lam = lam_ref[...]
    a = jnp.sum(lam[0:1, :] * lam[1:2, :], axis=1, keepdims=True)
    b = jnp.sum(lam[2:3, :] * lam[3:4, :], axis=1, keepdims=True)
    return jnp.exp(a) - jnp.exp(b) + lam_init


def _diff_prompt_kernel(qt_ref, k_ref, vt_ref, lam_ref, g_ref, o_ref, s_ref, *, lam_init, tk):
    j = pl.program_id(2)
    tq = QUERY_TILE
    hw = 2 * HEAD
    row_map = lax.broadcasted_iota(jnp.int32, (hw, 1), 0) // HEAD
    qt2 = []
    for hh in range(2):
        qt = qt_ref[0, hh * hw:(hh + 1) * hw, :]
        qt2.append(jnp.concatenate([jnp.where(row_map == 0, qt, 0.0), jnp.where(row_map == 1, qt, 0.0)], axis=1))

    def tile_rows(n):
        return pl.ds(pl.multiple_of(n * tk, tk), tk)

    def scores(n):
        rows = tile_rows(n)
        for hh in range(2):
            s_ref[hh] = _dot(k_ref[0, rows, hh * hw:(hh + 1) * hw], qt2[hh])

    def update(n, st, s, mask):
        rows = tile_rows(n)
        return tuple(_flash_update(st[hh], s[hh], vt_ref[0, hh * DIFF_VDIM:(hh + 1) * DIFF_VDIM, rows], None, mask)
                     for hh in range(2))

    def body(n, st):
        s_cur = [s_ref[hh] for hh in range(2)]
        scores(n + 1)
        return update(n, st, s_cur, None)

    init = _flash_init(DIFF_VDIM, 2 * tq)
    last = (j * tq) // tk
    scores(0)
    state = lax.fori_loop(0, last, body, (init, init))
    key = last * tk + lax.broadcasted_iota(jnp.int32, (tk, 2 * tq), 0)
    qry = j * tq + lax.broadcasted_iota(jnp.int32, (tk, 2 * tq), 1) % tq
    state = update(last, state, [s_ref[hh] for hh in range(2)], key <= qry)

    lam = _diff_lambda(lam_ref, lam_init)
    for hh in range(2):
        ot = _flash_finish(state[hh], DIFF_VDIM)
        d = (ot[:, :tq] - lam * ot[:, tq:]).T
        o_ref[0, :, hh * DIFF_VDIM:(hh + 1) * DIFF_VDIM] = _rms(d, g_ref[...]) * (1.0 - lam_init)


def _diff_prompt(qt, k, vt, lam, subln, lam_init, layer):
    _, b, s, _ = k.shape
    pw = 2 * DIFF_VDIM
    tq = QUERY_TILE
    tk = DIFF_KEY_TILE
    return pl.pallas_call(
        functools.partial(_diff_prompt_kernel, lam_init=lam_init, tk=tk),
        grid=(b, DIFF_HEADS // 2, s // tq),
        in_specs=[pl.BlockSpec((1, pw, tq), lambda i, h, j: (i, h, j)),
                  pl.BlockSpec((None, 1, s, pw), lambda i, h, j: (layer, i, 0, h)),
                  pl.BlockSpec((1, pw, s), lambda i, h, j: (i, h, 0)),
                  pl.BlockSpec((4, HEAD), lambda i, h, j: (0, 0)),
                  pl.BlockSpec((1, DIFF_VDIM), lambda i, h, j: (0, 0))],
        out_specs=pl.BlockSpec((1, tq, pw), lambda i, h, j: (i, j, h)),
        out_shape=jax.ShapeDtypeStruct((b, s, SEG), F32),
        scratch_shapes=[pltpu.VMEM((2, tk, 2 * tq), F32)],
        compiler_params=_cparams(("parallel", "parallel", "arbitrary")),
        name="diff_prompt",
    )(qt, k, vt, lam, subln)


def _merge_kernel(x_ref, om_ref, od_ref, oc_ref, nm_ref, wg_ref, bg_ref, wb_ref, wo_ref, y_ref):
    x = x_ref[...]
    d = x.shape[1]
    h = _rms(x, nm_ref[...]).astype(BF16)
    merged = jnp.zeros_like(x)
    for n, o_ref in enumerate((om_ref, od_ref, oc_ref)):
        sl = slice(n * d, (n + 1) * d)
        g = jax.nn.sigmoid(_dot(h, wg_ref[:, sl]) + bg_ref[:, sl])
        merged = merged + g * _dot(o_ref[...].astype(BF16), wb_ref[n])
    y_ref[...] = x + _dot(merged.astype(BF16), wo_ref[...])


def _merge(x2d, o_moba, o_diff, o_mem, norm_mix, w_gate, b_gate, w_branch, w_out, tm):
    n, d = x2d.shape
    tok = lambda i: (i, 0)
    full = lambda i: (0, 0)
    return pl.pallas_call(
        _merge_kernel,
        grid=(n // tm,),
        in_specs=[pl.BlockSpec((tm, d), tok),
                  pl.BlockSpec((tm, SEG), tok),
                  pl.BlockSpec((tm, SEG), tok),
                  pl.BlockSpec((tm, SEG), tok),
                  pl.BlockSpec((1, d), full),
                  pl.BlockSpec((d, 3 * d), full),
                  pl.BlockSpec((1, 3 * d), full),
                  pl.BlockSpec((3, SEG, d), lambda i: (0, 0, 0)),
                  pl.BlockSpec((d, d), full)],
        out_specs=pl.BlockSpec((tm, d), tok),
        out_shape=jax.ShapeDtypeStruct((n, d), F32),
        compiler_params=_cparams(("parallel",)),
        name="merge",
    )(x2d, o_moba, o_diff, o_mem, norm_mix, w_gate, b_gate, w_branch, w_out)


def _ffn_kernel(x_ref, nf_ref, wu_ref, wd_ref, y_ref, *, n_chunks):
    x = x_ref[...]
    hn = _rms(x, nf_ref[...]).astype(BF16)
    cw = wu_ref.shape[1] // n_chunks
    acc = x
    for c in range(n_chunks):
        u = jnp.square(jnp.maximum(_dot(hn, wu_ref[:, c * cw:(c + 1) * cw]), 0.0))
        acc = acc + _dot(u.astype(BF16), wd_ref[c * cw:(c + 1) * cw, :])
    y_ref[...] = acc


def _ffn(x2d, norm_ffn, w_up, w_down, tm):
    n, d = x2d.shape
    dff = w_up.shape[1]
    tok = lambda i: (i, 0)
    full = lambda i: (0, 0)
    return pl.pallas_call(
        functools.partial(_ffn_kernel, n_chunks=4),
        grid=(n // tm,),
        in_specs=[pl.BlockSpec((tm, d), tok),
                  pl.BlockSpec((1, d), full),
                  pl.BlockSpec((d, dff), full),
                  pl.BlockSpec((dff, d), full)],
        out_specs=pl.BlockSpec((tm, d), tok),
        out_shape=jax.ShapeDtypeStruct((n, d), F32),
        compiler_params=_cparams(("parallel",)),
        name="ffn",
    )(x2d, norm_ffn, w_up, w_down)


def _page_specs(layer, npg, block):
    def spec(jj):
        return pl.BlockSpec((None, None) + block,
                            lambda b, c, pt: (layer, pt[b, c * npg + jj]) + (0,) * len(block))
    return [spec(jj) for jj in range(npg)]


def _new_token_scores(qrows, knx, t_dec):
    tok = lax.broadcasted_iota(jnp.int32, (qrows.shape[0], 1), 0) % T_PAD
    return [jnp.where(tok >= t, jnp.sum(qrows * knx[t], axis=1, keepdims=True), NEG) for t in range(t_dec)]


def _diff_sample_kernel(pt_ref, q_ref, kn_ref, vn_ref, lam_ref, g_ref, *rest, npg, lam_init, t_dec):
    del pt_ref
    k_refs, v_refs = rest[:npg], rest[npg:2 * npg]
    o_ref, m_ref, l_ref, acc_ref = rest[2 * npg:]
    c = pl.program_id(1)
    nrow = DIFF_HEADS * 2 * T_PAD
    hw = 2 * HEAD

    def head_rows(x8, masked):
        lane_map = lax.broadcasted_iota(jnp.int32, (1, hw), 1) // HEAD
        out = []
        for hh in range(DIFF_HEADS):
            xs = x8[:, hh * hw:(hh + 1) * hw]
            for mp in range(2):
                out.append(jnp.where(lane_map == mp, xs, 0.0) if masked else xs)
        return jnp.concatenate(out, axis=0)

    qx = head_rows(q_ref[0] * (HEAD ** -0.5), True)

    @pl.when(c == 0)
    def _():
        m_ref[...] = jnp.full(m_ref.shape, NEG, F32)
        l_ref[...] = jnp.zeros(l_ref.shape, F32)
        acc_ref[...] = jnp.zeros(acc_ref.shape, F32)

    s = jnp.concatenate([_dot_nt(qx, k_refs[p][...]) for p in range(npg)], axis=1)
    row_head = lax.broadcasted_iota(jnp.int32, s.shape, 0) // (2 * T_PAD)
    col_head = lax.broadcasted_iota(jnp.int32, s.shape, 1) % DIFF_HEADS
    s = jnp.where(row_head == col_head, s, NEG)
    m_old = m_ref[...]
    m_new = jnp.maximum(m_old, jnp.max(s, axis=1, keepdims=True))
    alpha = jnp.exp(m_old - m_new)
    p_all = jnp.exp(s - m_new)
    pw = k_refs[0].shape[0]
    pv = _dot(p_all[:, :pw], v_refs[0][...])
    for p in range(1, npg):
        pv = pv + _dot(p_all[:, p * pw:(p + 1) * pw], v_refs[p][...])
    l_ref[...] = alpha * l_ref[...] + jnp.sum(p_all, axis=1, keepdims=True)
    acc_ref[...] = alpha * acc_ref[...] + pv
    m_ref[...] = m_new

    @pl.when(c == pl.num_programs(1) - 1)
    def _():
        kn, vn = kn_ref[0], vn_ref[0]
        knx = [head_rows(jnp.broadcast_to(kn[t:t + 1, :], (T_PAD, SEG)), False) for t in range(t_dec)]
        vnx = [head_rows(jnp.broadcast_to(vn[t:t + 1, :], (T_PAD, SEG)), False) for t in range(t_dec)]
        sn = _new_token_scores(qx, knx, t_dec)
        m0 = m_ref[...]
        m1 = m0
        for t in range(t_dec):
            m1 = jnp.maximum(m1, sn[t])
        a = jnp.exp(m0 - m1)
        l = a * l_ref[...]
        acc = a * acc_ref[...]
        for t in range(t_dec):
            pt = jnp.exp(sn[t] - m1)
            l = l + pt
            acc = acc + pt * vnx[t]
        o = acc / l
        lam = _diff_lambda(lam_ref, lam_init)
        for hh in range(DIFF_HEADS):
            r0 = 2 * hh * T_PAD
            d = o[r0:r0 + T_PAD, :] - lam * o[r0 + T_PAD:r0 + 2 * T_PAD, :]
            o_ref[0, :, hh * DIFF_VDIM:(hh + 1) * DIFF_VDIM] = _rms(d, g_ref[...]) * (1.0 - lam_init)


def _diff_sample(page_table, q, kn, vn, lam, subln, cache_k, cache_v, layer, lam_init, t_dec):
    bd = q.shape[0]
    n_pages = page_table.shape[1]
    rows = cache_k.shape[2]
    npg = min(PAGES_PER_STEP, n_pages)
    nrow = DIFF_HEADS * 2 * T_PAD
    tokspec = pl.BlockSpec((1, T_PAD, SEG), lambda b, c, pt: (b, 0, 0))
    grid_spec = pltpu.PrefetchScalarGridSpec(
        num_scalar_prefetch=1,
        grid=(bd, n_pages // npg),
        in_specs=[tokspec, tokspec, tokspec,
                  pl.BlockSpec((4, HEAD), lambda b, c, pt: (0, 0)),
                  pl.BlockSpec((1, DIFF_VDIM), lambda b, c, pt: (0, 0))]
                 + _page_specs(layer, npg, (rows, 128)) + _page_specs(layer, npg, (rows, 128)),
        out_specs=tokspec,
        scratch_shapes=[pltpu.VMEM((nrow, 1), F32),
                        pltpu.VMEM((nrow, 1), F32),
                        pltpu.VMEM((nrow, DIFF_VDIM), F32)])
    return pl.pallas_call(
        functools.partial(_diff_sample_kernel, npg=npg, lam_init=lam_init, t_dec=t_dec),
        grid_spec=grid_spec,
        out_shape=jax.ShapeDtypeStruct((bd, T_PAD, SEG), F32),
        compiler_params=_cparams(("parallel", "arbitrary")),
        name="diff_sample",
    )(page_table, q, kn, vn, lam, subln, *([cache_k] * npg), *([cache_v] * npg))


def _group_query(q8):
    rows = lax.broadcasted_iota(jnp.int32, (MOBA_HEADS * T_PAD, SEG), 0) // T_PAD
    lanes = lax.broadcasted_iota(jnp.int32, (MOBA_HEADS * T_PAD, SEG), 1) // HEAD
    return jnp.where(rows == lanes, jnp.concatenate([q8] * MOBA_HEADS, axis=0), 0.0)


def _moba_sample_kernel(pt_ref, q_ref, kn_ref, vn_ref, *rest, npg, t_dec, nblk):
    del pt_ref
    k_refs, v_refs = rest[:npg], rest[npg:2 * npg]
    o_ref, ks_ref, pm_ref, pl_ref, po_ref = rest[2 * npg:]
    c = pl.program_id(1)
    qg = _group_query(q_ref[0] * (HEAD ** -0.5))
    pg = k_refs[0].shape[2]
    ppb = MOBA_BLOCK // pg
    bpc = npg // ppb
    lane = lax.broadcasted_iota(jnp.int32, (1, 128), 1)

    @pl.when(c == 0)
    def _():
        ks_ref[...] = jnp.zeros(ks_ref.shape, F32)
        pm_ref[...] = jnp.zeros(pm_ref.shape, F32)
        pl_ref[...] = jnp.zeros(pl_ref.shape, F32)

    for bb in range(bpc):
        kts = [k_refs[bb * ppb + i][...].reshape(SEG, pg) for i in range(ppb)]
        vts = [v_refs[bb * ppb + i][...].reshape(SEG, pg) for i in range(ppb)]
        s = jnp.concatenate([_dot(qg, kt) for kt in kts], axis=1)
        m = jnp.max(s, axis=1, keepdims=True)
        p = jnp.exp(s - m)
        l = jnp.sum(p, axis=1, keepdims=True)
        o = _dot_nt(p[:, :pg], vts[0])
        ksum = kts[0]
        for i in range(1, ppb):
            o = o + _dot_nt(p[:, i * pg:(i + 1) * pg], vts[i])
            ksum = ksum + kts[i]
        ksum = jnp.sum(ksum, axis=1, keepdims=True)
        n = c * bpc + bb
        ks_ref[...] = jnp.where(lane == n, ksum, ks_ref[...])
        pm_ref[...] = jnp.where(lane == n, m, pm_ref[...])
        pl_ref[...] = jnp.where(lane == n, l, pl_ref[...])
        po_ref[n] = o

    @pl.when(c == pl.num_programs(1) - 1)
    def _():
        gate = _dot_split(qg, ks_ref[...])
        sel = _top_k_mask(gate, lane < nblk, 1)
        kn, vn = kn_ref[0], vn_ref[0]
        sn = _new_token_scores(qg, [kn[t:t + 1, :] for t in range(t_dec)], t_dec)
        pm = jnp.where(sel, pm_ref[...], NEG)
        mt = jnp.max(pm, axis=1, keepdims=True)
        for t in range(t_dec):
            mt = jnp.maximum(mt, sn[t])
        w = jnp.where(sel, jnp.exp(pm - mt), 0.0)
        l = jnp.sum(w * pl_ref[...], axis=1, keepdims=True)
        o = jnp.zeros((MOBA_HEADS * T_PAD, SEG), F32)
        for n in range(nblk):
            o = o + w[:, n:n + 1] * po_ref[n]
        for t in range(t_dec):
            pt = jnp.exp(sn[t] - mt)
            l = l + pt
            o = o + pt * vn[t:t + 1, :]
        o = o / l
        lane_head = lax.broadcasted_iota(jnp.int32, (T_PAD, SEG), 1) // HEAD
        out = jnp.zeros((T_PAD, SEG), F32)
        for hh in range(MOBA_HEADS):
            out = jnp.where(lane_head == hh, o[hh * T_PAD:(hh + 1) * T_PAD, :], out)
        o_ref[0] = out


def _moba_sample(page_table, q, kn, vn, cache_kt, cache_vt, layer, t_dec):
    bd = q.shape[0]
    n_pages = page_table.shape[1]
    page = cache_kt.shape[4]
    npg = min(PAGES_PER_STEP, n_pages)
    nblk = n_pages * page // MOBA_BLOCK
    nrow = MOBA_HEADS * T_PAD
    tokspec = pl.BlockSpec((1, T_PAD, SEG), lambda b, c, pt: (b, 0, 0))
    blk = (MOBA_HEADS, HEAD, page)
    grid_spec = pltpu.PrefetchScalarGridSpec(
        num_scalar_prefetch=1,
        grid=(bd, n_pages // npg),
        in_specs=[tokspec, tokspec, tokspec] + _page_specs(layer, npg, blk) + _page_specs(layer, npg, blk),
        out_specs=tokspec,
        scratch_shapes=[pltpu.VMEM((SEG, 128), F32),
                        pltpu.VMEM((nrow, 128), F32),
                        pltpu.VMEM((nrow, 128), F32),
                        pltpu.VMEM((nblk, nrow, SEG), F32)])
    return pl.pallas_call(
        functools.partial(_moba_sample_kernel, npg=npg, t_dec=t_dec, nblk=nblk),
        grid_spec=grid_spec,
        out_shape=jax.ShapeDtypeStruct((bd, T_PAD, SEG), F32),
        compiler_params=_cparams(("parallel", "arbitrary")),
        name="moba_sample",
    )(page_table, q, kn, vn, *([cache_kt] * npg), *([cache_vt] * npg))


def _rope_tables(pos):
    half = HEAD // 2
    inv = ROPE_THETA ** (-2.0 * jnp.arange(half, dtype=F32) / HEAD)
    ang = pos.astype(F32)[:, None] * inv[None, :]
    cos, sin = jnp.cos(ang), jnp.sin(ang)
    return jnp.concatenate([cos] * 4, axis=1), jnp.concatenate([-sin, sin] * 2, axis=1)


def _block_diag(width, group, value, dtype):
    i = jnp.arange(width) // group
    return jnp.where(i[:, None] == i[None, :], value, 0.0).astype(dtype)


def kernel(x_prompt, x_sample, cache_moba_k, cache_moba_v, cache_diff_k, cache_diff_v, cache_mem_k, cache_mem_v, page_table, mem_prompt, norm_mix, w_in, moba_qn, moba_kn, diff_qn, diff_kn, diff_lambda, diff_subln, mem_norm, w_mem_kv, mem_qn, mem_kn, w_branch, w_gate, b_gate, w_out, norm_ffn, w_up, w_down):
    depth = w_in.shape[0]
    b, s, d = x_prompt.shape
    bd, t_dec, _ = x_sample.shape
    n_pool, page = cache_moba_k.shape[1], cache_moba_k.shape[2]
    n_pages = page_table.shape[1]
    past = n_pages * page
    assert d == 2 * SEG and w_in.shape[2] == 7 * SEG
    assert s % TOKEN_TILE == 0 and past % MOBA_BLOCK == 0 and MOBA_BLOCK % page == 0 and page % 128 == 0
    assert t_dec <= T_PAD and n_pages % min(PAGES_PER_STEP, n_pages) == 0
    assert MOBA_TOPK <= past // MOBA_BLOCK <= 128

    tm = TOKEN_TILE
    cos_p, sin_p = _rope_tables(jnp.arange(s, dtype=jnp.int32))
    cos_s, sin_s = _rope_tables(past + jnp.arange(T_PAD, dtype=jnp.int32))
    cos_s, sin_s = jnp.tile(cos_s, (bd, 1)), jnp.tile(sin_s, (bd, 1))
    gmat = _block_diag(SEG, HEAD, 1.0 / HEAD, F32)

    w_in_b, w_mem_b = w_in.astype(BF16), w_mem_kv.astype(BF16)
    w_gate_b, w_branch_b, w_out_b = w_gate.astype(BF16), w_branch.astype(BF16), w_out.astype(BF16)
    w_up_b, w_down_b = w_up.astype(BF16), w_down.astype(BF16)

    c_mkt = jnp.transpose(cache_moba_k, (0, 1, 3, 4, 2))
    c_mvt = jnp.transpose(cache_moba_v, (0, 1, 3, 4, 2))
    c_dk = cache_diff_k.reshape(depth, n_pool, page * DIFF_HEADS, 2 * HEAD)
    c_dv = cache_diff_v.reshape(depth, n_pool, page * DIFF_HEADS, DIFF_VDIM)
    c_ck = cache_mem_k.reshape(depth, bd, -1, MEM_DIM)
    c_cv = cache_mem_v.reshape(depth, bd, -1, MEM_DIM)

    xp = x_prompt.reshape(b * s, d)
    xs = jnp.pad(x_sample, ((0, 0), (0, T_PAD - t_dec), (0, 0))).reshape(bd * T_PAD, d)
    ns = bd * T_PAD
    outs = [[] for _ in range(10)]
    prev_kv = ()
    for l in range(depth):
        lam_init = 0.8 - 0.6 * math.exp(-0.3 * l)
        gains = jnp.concatenate([jnp.tile(moba_qn[l], 8)[None], jnp.tile(moba_kn[l], 8)[None],
                                 jnp.tile(diff_qn[l], 8)[None], jnp.tile(diff_kn[l], 8)[None],
                                 jnp.tile(mem_qn[l], 4)[None], jnp.zeros((3, SEG), F32)], axis=0)
        nm, nf = norm_mix[l][None], norm_ffn[l][None]
        subln = diff_subln[l][None]

        mk, mkt, mvt, mqt, dk, dv, dqt, dvt, cq = _project_prompt(
            xp, b, s, depth, l, prev_kv, nm, w_in_b[l], gains, cos_p, sin_p, gmat)
        prev_kv = (mkt, mvt, dk, dv)
        o_moba = _moba_prompt(mqt, mk, mvt, l)
        o_diff = _diff_prompt(dqt, dk, dvt, diff_lambda[l], subln, lam_init, l)
        m_k, m_v = _mem_kv(mem_prompt, mem_norm[l][None], w_mem_b[l], mem_kn[l][None])
        o_mem = _mem_attend(cq, m_k, m_v, 1024 if s % 1024 == 0 else TOKEN_TILE)
        xp = _merge(xp, o_moba.reshape(b * s, SEG), o_diff.reshape(b * s, SEG), o_mem.reshape(b * s, SEG),
                    nm, w_gate_b[l], b_gate[l][None], w_branch_b[l], w_out_b[l], tm)
        xp = _ffn(xp, nf, w_up_b[l], w_down_b[l], tm)
        for lst, a in zip(outs[4:6], (m_k, m_v)):
            lst.append(a)

        smq, smk, smv, sdq, sdk, sdv, scq = _project(xs, nm, w_in_b[l], gains, cos_s, sin_s, gmat)
        t3 = lambda a: a.reshape(bd, T_PAD, SEG)
        so_moba = _moba_sample(page_table, t3(smq), t3(smk), t3(smv), c_mkt, c_mvt, l, t_dec)
        so_diff = _diff_sample(page_table, t3(sdq), t3(sdk), t3(sdv), diff_lambda[l], subln,
                               c_dk, c_dv, l, lam_init, t_dec)
        so_mem = _mem_sample(t3(scq), c_ck, c_cv, l)
        xs = _merge(xs, so_moba.reshape(ns, SEG), so_diff.reshape(ns, SEG), so_mem.reshape(ns, SEG),
                    nm, w_gate_b[l], b_gate[l][None], w_branch_b[l], w_out_b[l], ns)
        xs = _ffn(xs, nf, w_up_b[l], w_down_b[l], ns)
        for lst, a in zip(outs[6:], (smk, smv, sdk, sdv)):
            lst.append(t3(a)[:, :t_dec])

    st = lambda lst, shape: jnp.stack(lst).reshape((depth,) + shape)
    mkt, mvt, dk, dv = prev_kv
    tr = lambda a: jnp.transpose(a, (0, 1, 4, 2, 3))
    return (xp.reshape(b, s, d),
            xs.reshape(bd, T_PAD, d)[:, :t_dec],
            tr(mkt), tr(mvt),
            dk.reshape(depth, b, s, DIFF_HEADS, 2 * HEAD), dv.reshape(depth, b, s, DIFF_HEADS, DIFF_VDIM),
            st(outs[4], (b, -1, MEM_HEADS, MEM_DIM)), st(outs[5], (b, -1, MEM_HEADS, MEM_DIM)),
            st(outs[6], (bd, t_dec, 8, HEAD)), st(outs[7], (bd, t_dec, 8, HEAD)),
            st(outs[8], (bd, t_dec, DIFF_HEADS, 2 * HEAD)), st(outs[9], (bd, t_dec, DIFF_HEADS, DIFF_VDIM)))
```
